```python
import functools
import jax
import jax.numpy as jnp
from jax import lax
import numpy as np

D_MODEL = 2048
BATCH = 4
SEQ = 2048
DEPTH = 2
DEC_BATCH = 32
DEC_SEQ = 8
PAST_LEN = 8192
PAGE_SIZE = 128

HEAD_DIM = 128
D_MIX = D_MODEL
CONV_CH = D_MIX // 4
CONV_W = 31
MLSTM_HEADS = (D_MIX // 4) // HEAD_DIM
MLSTM_W = MLSTM_HEADS * HEAD_DIM
MLSTM_CHUNK = 64
ATT_HEADS = (D_MIX // 2) // HEAD_DIM
ATT_W = ATT_HEADS * HEAD_DIM
Q_BLOCK = 128
N_EXPERTS = 64
N_EXPERT_GROUPS = 8
TOPK_GROUPS = 4
TOP_K = 8
D_EXPERT = D_MODEL // 4
ROUTED_SCALE = 2.5
DISPATCH_BLOCK = 128
NORM_EPS = 1e-6

IN_SIZES = (CONV_CH, CONV_CH, MLSTM_W, MLSTM_W, MLSTM_W, MLSTM_W, MLSTM_HEADS, MLSTM_HEADS,
            ATT_W, ATT_W, ATT_W, ATT_HEADS)
N_IN = sum(IN_SIZES)

kernel_name = "hymba_conv_mlstm_fox_moe_decoder_step"

F32 = jnp.float32


def _rms_norm(x, g):
    xf = x.astype(F32)
    y = xf * lax.rsqrt(jnp.mean(xf * xf, axis=-1, keepdims=True) + NORM_EPS)
    return (y * g.astype(F32)).astype(x.dtype)


def _layer_norm(x, g, b):
    xf = x.astype(F32)
    mu = jnp.mean(xf, axis=-1, keepdims=True)
    var = jnp.mean(jnp.square(xf - mu), axis=-1, keepdims=True)
    y = (xf - mu) * lax.rsqrt(var + NORM_EPS)
    return (y * g.astype(F32) + b.astype(F32)).astype(x.dtype)


def _split_cols(u):
    outs, off = [], 0
    for s in IN_SIZES:
        outs.append(u[..., off:off + s])
        off += s
    return outs


def _conv_module(a, gate, buf, dw, db, ln_g, ln_b):
    u = a * jax.nn.sigmoid(gate)
    full = jnp.concatenate([buf.astype(u.dtype), u], axis=1)
    y = lax.conv_general_dilated(full, dw[:, None, :].astype(u.dtype), (1,), 'VALID',
                                 dimension_numbers=('NWC', 'WIO', 'NWC'),
                                 feature_group_count=CONV_CH) + db.astype(u.dtype)
    y = jax.nn.silu(_layer_norm(y, ln_g, ln_b))
    return y, full[:, full.shape[1] - (CONV_W - 1):, :]


def _mlstm_chunk(carry, xs):
    C, n, m = carry
    q, k, v, ig, lf = xs
    c = q.shape[2]
    b = jnp.cumsum(lf, axis=-1)
    causal = jnp.tril(jnp.ones((c, c), bool))
    D = jnp.where(causal, b[..., :, None] - b[..., None, :] + ig[..., None, :], -jnp.inf)
    inter = b + m[..., None]
    m_t = jnp.maximum(inter, jnp.max(D, axis=-1))
    Dw = jnp.exp(D - m_t[..., None])
    iw = jnp.exp(inter - m_t)
    S = jnp.einsum('bhtd,bhsd->bhts', q, k) * Dw
    num = jnp.einsum('bhts,bhse->bhte', S, v) + iw[..., None] * jnp.einsum('bhtd,bhde->bhte', q, C)
    den = jnp.sum(S, axis=-1) + iw * jnp.einsum('bhtd,bhd->bht', q, n)
    h = num / jnp.maximum(jnp.abs(den), jnp.exp(-m_t))[..., None]
    w_end = Dw[..., -1, :]
    f_end = iw[..., -1]
    C_new = f_end[..., None, None] * C + jnp.einsum('bhs,bhsd,bhse->bhde', w_end, k, v)
    n_new = f_end[..., None] * n + jnp.einsum('bhs,bhsd->bhd', w_end, k)
    return (C_new, n_new, m_t[..., -1]), h


def _mlstm_module(q, k, v, i_pre, f_pre, o_pre, C0, n0, m0):
    B, L = q.shape[:2]
    c = MLSTM_CHUNK if L % MLSTM_CHUNK == 0 else L
    nc = L // c

    def chunks(t):
        return t.astype(F32).reshape(B, nc, c, MLSTM_HEADS, -1).transpose(1, 0, 3, 2, 4)

    qc = chunks(q)
    kc = chunks(k) * (HEAD_DIM ** -0.5)
    vc = chunks(v)
    igc = chunks(i_pre)[..., 0]
    lfc = jax.nn.log_sigmoid(chunks(f_pre)[..., 0])
    (C, n, m), h = lax.scan(_mlstm_chunk, (C0.astype(F32), n0.astype(F32), m0.astype(F32)),
                            (qc, kc, vc, igc, lfc))
    h = h.transpose(1, 0, 3, 2, 4).reshape(B, L, MLSTM_W)
    out = (jax.nn.sigmoid(o_pre.astype(F32)) * h).astype(q.dtype)
    return out, C.astype(C0.dtype), n.astype(n0.dtype), m.astype(m0.dtype)


def _fox_prompt(q, k, v, lf):
    B, L = q.shape[:2]
    nb = L // Q_BLOCK
    scale = HEAD_DIM ** -0.5
    Fk = jnp.cumsum(lf, axis=1).transpose(0, 2, 1)
    kf = k.astype(F32)
    kpos = jnp.arange(L)

    def blk(i):
        qi = lax.dynamic_slice_in_dim(q, i * Q_BLOCK, Q_BLOCK, axis=1).astype(F32) * scale
        Fq = lax.dynamic_slice_in_dim(Fk, i * Q_BLOCK, Q_BLOCK, axis=2)
        s = jnp.einsum('bqhd,bkhd->bhqk', qi, kf) + Fq[..., None] - Fk[:, :, None, :]
        qpos = i * Q_BLOCK + jnp.arange(Q_BLOCK)
        s = jnp.where(kpos[None, :] <= qpos[:, None], s, -jnp.inf)
        p = jax.nn.softmax(s, axis=-1).astype(v.dtype)
        return jnp.einsum('bhqk,bkhd->bqhd', p, v)

    out = lax.map(blk, jnp.arange(nb))
    return out.transpose(1, 0, 2, 3, 4).reshape(B, L, ATT_W)


def _fox_sample(q, k, v, lf, cache_k, cache_v, cache_logf, layer, page_table):
    Bd, S = q.shape[:2]
    scale = HEAD_DIM ** -0.5
    causal = jnp.tril(jnp.ones((S, S), bool))

    def one(args):
        qi, ki, vi, lfi, pt = args
        kp = cache_k[layer, pt].reshape(-1, ATT_HEADS, HEAD_DIM)
        vp = cache_v[layer, pt].reshape(-1, ATT_HEADS, HEAD_DIM)
        lfp = cache_logf[layer, pt].reshape(-1, ATT_HEADS).astype(F32)
        fn = jnp.cumsum(lfi, axis=0)
        suf = lax.cumsum(lfp, axis=0, reverse=True) - lfp
        qf = qi.astype(F32) * scale
        s_past = jnp.einsum('qhd,khd->hqk', qf, kp.astype(F32)) + fn.T[:, :, None] + suf.T[:, None, :]
        s_new = jnp.einsum('qhd,khd->hqk', qf, ki.astype(F32)) + fn.T[:, :, None] - fn.T[:, None, :]
        s_new = jnp.where(causal, s_new, -jnp.inf)
        p = jax.nn.softmax(jnp.concatenate([s_past, s_new], axis=-1), axis=-1).astype(vi.dtype)
        vall = jnp.concatenate([vp.astype(vi.dtype), vi], axis=0)
        return jnp.einsum('hqk,khd->qhd', p, vall)

    out = lax.map(one, (q, k, v, lf, page_table))
    return out.reshape(Bd, S, ATT_W)


def _swiglu(x, wg, wu, wd):
    return (jax.nn.silu(x @ wg) * (x @ wu)) @ wd


def _routed_experts(x2, idx, gate, w_gate, w_up, w_down):
    T, D = x2.shape
    A = T * TOP_K
    e_flat = idx.reshape(A)
    tok_flat = jnp.repeat(jnp.arange(T, dtype=jnp.int32), TOP_K)
    g_flat = gate.reshape(A)
    order = jnp.argsort(e_flat)
    e_sorted = e_flat[order]
    counts = jnp.bincount(e_flat, length=N_EXPERTS)
    padded = (counts + DISPATCH_BLOCK - 1) // DISPATCH_BLOCK * DISPATCH_BLOCK
    starts = jnp.cumsum(counts) - counts
    pends = jnp.cumsum(padded)
    pstarts = pends - padded
    dest = pstarts[e_sorted] + jnp.arange(A) - starts[e_sorted]
    nb = -(-A // DISPATCH_BLOCK) + N_EXPERTS
    P = nb * DISPATCH_BLOCK
    row_tok = jnp.full((P,), T, jnp.int32).at[dest].set(tok_flat[order])
    row_gate = jnp.zeros((P,), gate.dtype).at[dest].set(g_flat[order])
    block_e = jnp.minimum(jnp.searchsorted(pends, jnp.arange(nb) * DISPATCH_BLOCK, side='right'),
                          N_EXPERTS - 1)
    x_pad = jnp.concatenate([x2, jnp.zeros((1, D), x2.dtype)], axis=0)

    def body(y, blk):
        rows, g, e = blk
        xb = x_pad[rows]
        hb = jax.nn.silu(xb @ w_gate[e]) * (xb @ w_up[e])
        yb = (hb @ w_down[e]) * g[:, None]
        return y.at[rows].add(yb), None

    y, _ = lax.scan(body, jnp.zeros((T + 1, D), x2.dtype),
                    (row_tok.reshape(nb, DISPATCH_BLOCK), row_gate.reshape(nb, DISPATCH_BLOCK), block_e))
    return y[:T]


def _moe(x2, w_router, router_bias, w_exp_gate, w_exp_up, w_exp_down, w_sh_gate, w_sh_up, w_sh_down):
    T = x2.shape[0]
    scores = jax.nn.sigmoid(jnp.dot(x2.astype(F32), w_router.astype(F32)))
    sel = scores + router_bias.astype(F32)
    grp_score = jnp.sum(lax.top_k(sel.reshape(T, N_EXPERT_GROUPS, -1), 2)[0], axis=-1)
    _, top_g = lax.top_k(grp_score, TOPK_GROUPS)
    gmask = jnp.any(top_g[..., None] == jnp.arange(N_EXPERT_GROUPS), axis=1)
    sel = jnp.where(jnp.repeat(gmask, N_EXPERTS // N_EXPERT_GROUPS, axis=1), sel, -jnp.inf)
    _, idx = lax.top_k(sel, TOP_K)
    w = jnp.take_along_axis(scores, idx, axis=1)
    w = w / jnp.sum(w, axis=-1, keepdims=True) * ROUTED_SCALE
    routed = _routed_experts(x2, idx, w.astype(x2.dtype), w_exp_gate, w_exp_up, w_exp_down)
    return _swiglu(x2, w_sh_gate, w_sh_up, w_sh_down) + routed


def _layer(x, c, p, attn_fn, conv_buf, mC, mn, mm):
    B, L, D = x.shape
    mod = (jax.nn.silu(c) @ p['w_ada'] + p['b_ada'])[:, None, :]
    sh1, sc1, gt1, sh2, sc2, gt2 = jnp.split(mod, 6, axis=-1)
    h = _rms_norm(x, p['g_pre_mix']) * (1 + sc1) + sh1
    u = h @ p['w_in']
    ca, cg, mq, mk, mv, mo, mi, mf, aq, ak, av, af = _split_cols(u)
    y_conv, conv_new = _conv_module(ca, cg, conv_buf, p['conv_dw'], p['conv_db'],
                                    p['conv_ln_g'], p['conv_ln_b'])
    y_ml, C, n, m = _mlstm_module(mq, mk, mv, mi + p['mlstm_i_bias'], mf + p['mlstm_f_bias'], mo,
                                  mC, mn, mm)
    aq = aq.reshape(B, L, ATT_HEADS, HEAD_DIM)
    ak = ak.reshape(B, L, ATT_HEADS, HEAD_DIM)
    av = av.reshape(B, L, ATT_HEADS, HEAD_DIM)
    a_lf = jax.nn.log_sigmoid((af + p['fox_f_bias']).astype(F32))
    y_att = attn_fn(aq, ak, av, a_lf)
    mix = jnp.concatenate([y_conv, y_ml, y_att], axis=-1) @ p['w_out']
    x = x + gt1 * _rms_norm(mix, p['g_post_mix'])
    h2 = _rms_norm(x, p['g_pre_ffn']) * (1 + sc2) + sh2
    f = _moe(h2.reshape(B * L, D), p['w_router'], p['router_bias'], p['w_exp_gate'], p['w_exp_up'],
             p['w_exp_down'], p['w_sh_gate'], p['w_sh_up'], p['w_sh_down']).reshape(B, L, D)
    x = x + gt2 * _rms_norm(f, p['g_post_ffn'])
    return x, ak, av, a_lf.astype(x.dtype), conv_new, C, n, m


def setup_inputs(seed: int = 0) -> dict:
    key = jax.random.key(seed)
    ks = list(jax.random.split(key, 48))
    it = iter(ks)

    def nrm(shape, scale):
        return jax.random.normal(next(it), shape, F32) * scale

    def uni(shape, lo, hi):
        return jax.random.uniform(next(it), shape, F32, lo, hi)

    n_pages = PAST_LEN // PAGE_SIZE
    n_used = DEC_BATCH * n_pages
    n_phys = n_used + max(1, n_used // 4)
    D = D_MODEL
    inp = {}
    inp['x_prompt'] = nrm((BATCH, SEQ, D), 1.0)
    inp['x_sample'] = nrm((DEC_BATCH, DEC_SEQ, D), 1.0)
    inp['cache_k'] = nrm((DEPTH, n_phys, PAGE_SIZE, ATT_HEADS, HEAD_DIM), 1.0)
    inp['cache_v'] = nrm((DEPTH, n_phys, PAGE_SIZE, ATT_HEADS, HEAD_DIM), 1.0)
    inp['cache_logf'] = jax.nn.log_sigmoid(3.0 + nrm((DEPTH, n_phys, PAGE_SIZE, ATT_HEADS), 1.0))
    inp['state_conv'] = nrm((DEPTH, DEC_BATCH, CONV_W - 1, CONV_CH), 0.5)
    inp['state_mlstm_C'] = nrm((DEPTH, DEC_BATCH, MLSTM_HEADS, HEAD_DIM, HEAD_DIM), 0.1)
    inp['state_mlstm_n'] = nrm((DEPTH, DEC_BATCH, MLSTM_HEADS, HEAD_DIM), 0.1)
    inp['state_mlstm_m'] = nrm((DEPTH, DEC_BATCH, MLSTM_HEADS), 1.0)
    inp['page_table'] = jax.random.permutation(next(it), n_phys)[:n_used].reshape(
        DEC_BATCH, n_pages).astype(jnp.int32)
    inp['c_prompt'] = nrm((BATCH, D), 1.0)
    inp['c_sample'] = nrm((DEC_BATCH, D), 1.0)
    inp['w_ada'] = nrm((DEPTH, D, 6 * D), 0.3 * D ** -0.5)
    inp['b_ada'] = nrm((DEPTH, 6 * D), 0.02)
    inp['g_pre_mix'] = 1.0 + nrm((DEPTH, D), 0.05)
    inp['g_post_mix'] = 1.0 + nrm((DEPTH, D), 0.05)
    inp['g_pre_ffn'] = 1.0 + nrm((DEPTH, D), 0.05)
    inp['g_post_ffn'] = 1.0 + nrm((DEPTH, D), 0.05)
    inp['w_in'] = nrm((DEPTH, D, N_IN), D ** -0.5)
    inp['w_out'] = nrm((DEPTH, D_MIX, D), D_MIX ** -0.5)
    inp['conv_dw'] = nrm((DEPTH, CONV_W, CONV_CH), CONV_W ** -0.5)
    inp['conv_db'] = nrm((DEPTH, CONV_CH), 0.02)
    inp['conv_ln_g'] = 1.0 + nrm((DEPTH, CONV_CH), 0.05)
    inp['conv_ln_b'] = nrm((DEPTH, CONV_CH), 0.02)
    inp['mlstm_i_bias'] = nrm((DEPTH, MLSTM_HEADS), 0.1)
    inp['mlstm_f_bias'] = uni((DEPTH, MLSTM_HEADS), 3.0, 6.0)
    inp['fox_f_bias'] = uni((DEPTH, ATT_HEADS), 1.0, 5.0)
    inp['w_router'] = nrm((DEPTH, D, N_EXPERTS), D ** -0.5)
    inp['router_bias'] = nrm((DEPTH, N_EXPERTS), 0.01)
    inp['w_exp_gate'] = nrm((DEPTH, N_EXPERTS, D, D_EXPERT), D ** -0.5)
    inp['w_exp_up'] = nrm((DEPTH, N_EXPERTS, D, D_EXPERT), D ** -0.5)
    inp['w_exp_down'] = nrm((DEPTH, N_EXPERTS, D_EXPERT, D), D_EXPERT ** -0.5)
    inp['w_sh_gate'] = nrm((DEPTH, D, D_EXPERT), D ** -0.5)
    inp['w_sh_up'] = nrm((DEPTH, D, D_EXPERT), D ** -0.5)
    inp['w_sh_down'] = nrm((DEPTH, D_EXPERT, D), D_EXPERT ** -0.5)
    return inp


def reference(x_prompt, x_sample, cache_k, cache_v, cache_logf, state_conv, state_mlstm_C,
              state_mlstm_n, state_mlstm_m, page_table, c_prompt, c_sample, w_ada, b_ada,
              g_pre_mix, g_post_mix, g_pre_ffn, g_post_ffn, w_in, w_out, conv_dw, conv_db,
              conv_ln_g, conv_ln_b, mlstm_i_bias, mlstm_f_bias, fox_f_bias, w_router, router_bias,
              w_exp_gate, w_exp_up, w_exp_down, w_sh_gate, w_sh_up, w_sh_down):
    Bp = x_prompt.shape[0]
    dt = x_prompt.dtype
    xp, xs = x_prompt, x_sample
    kp_l, vp_l, lfp_l, ks_l, vs_l, lfs_l = [], [], [], [], [], []
    convp_l, convs_l, Cp_l, np_l, mp_l, Cs_l, ns_l, ms_l = [], [], [], [], [], [], [], []
    for l in range(DEPTH):
        p = dict(w_ada=w_ada[l], b_ada=b_ada[l], g_pre_mix=g_pre_mix[l], g_post_mix=g_post_mix[l],
                 g_pre_ffn=g_pre_ffn[l], g_post_ffn=g_post_ffn[l], w_in=w_in[l], w_out=w_out[l],
                 conv_dw=conv_dw[l], conv_db=conv_db[l], conv_ln_g=conv_ln_g[l], conv_ln_b=conv_ln_b[l],
                 mlstm_i_bias=mlstm_i_bias[l], mlstm_f_bias=mlstm_f_bias[l], fox_f_bias=fox_f_bias[l],
                 w_router=w_router[l], router_bias=router_bias[l], w_exp_gate=w_exp_gate[l],
                 w_exp_up=w_exp_up[l], w_exp_down=w_exp_down[l], w_sh_gate=w_sh_gate[l],
                 w_sh_up=w_sh_up[l], w_sh_down=w_sh_down[l])
        xp, kp, vp, lfp, convp, Cp, np_, mp = _layer(
            xp, c_prompt, p, _fox_prompt,
            jnp.zeros((Bp, CONV_W - 1, CONV_CH), dt),
            jnp.zeros((Bp, MLSTM_HEADS, HEAD_DIM, HEAD_DIM), dt),
            jnp.zeros((Bp, MLSTM_HEADS, HEAD_DIM), dt),
            jnp.zeros((Bp, MLSTM_HEADS), dt))
        sample_attn = functools.partial(_fox_sample, cache_k=cache_k, cache_v=cache_v,
                                        cache_logf=cache_logf, layer=l, page_table=page_table)
        xs, ks, vs, lfs, convs, Cs, ns, ms = _layer(
            xs, c_sample, p, sample_attn, state_conv[l], state_mlstm_C[l], state_mlstm_n[l],
            state_mlstm_m[l])
        kp_l.append(kp); vp_l.append(vp); lfp_l.append(lfp)
        ks_l.append(ks); vs_l.append(vs); lfs_l.append(lfs)
        convp_l.append(convp); convs_l.append(convs)
        Cp_l.append(Cp); np_l.append(np_); mp_l.append(mp)
        Cs_l.append(Cs); ns_l.append(ns); ms_l.append(ms)
    return (xp, xs,
            jnp.stack(kp_l), jnp.stack(vp_l), jnp.stack(lfp_l),
            jnp.stack(ks_l), jnp.stack(vs_l), jnp.stack(lfs_l),
            jnp.stack(convp_l), jnp.stack(convs_l),
            jnp.stack(Cp_l), jnp.stack(np_l), jnp.stack(mp_l),
            jnp.stack(Cs_l), jnp.stack(ns_l), jnp.stack(ms_l))
```

```python
import functools

import jax
import jax.numpy as jnp
from jax import lax
from jax.experimental import pallas as pl
from jax.experimental.pallas import tpu as pltpu

F32 = jnp.float32
BF16 = jnp.bfloat16
HI = lax.Precision.HIGHEST

HEAD_DIM = 128
CONV_W = 31
MLSTM_CHUNK = 64
N_EXPERT_GROUPS = 8
TOPK_GROUPS = 4
TOP_K = 8
ROUTED_SCALE = 2.5
NORM_EPS = 1e-6

LANES = 128
GATE_COLS = 128
EXPERT_BLOCK = 256
VMEM_LIMIT = 56 * 1024 * 1024


def _params(sem):
    return pltpu.CompilerParams(dimension_semantics=sem, vmem_limit_bytes=VMEM_LIMIT)


def _log_sigmoid(x):
    return jnp.minimum(x, 0.0) - jnp.log1p(jnp.exp(-jnp.abs(x)))


def _silu(x):
    return x * jax.nn.sigmoid(x)


def _rms(x):
    return x * lax.rsqrt(jnp.mean(x * x, axis=-1, keepdims=True) + NORM_EPS)


def _dot(a, b):
    return jnp.dot(a, b, preferred_element_type=F32)


def _dot_nt(a, b):
    return lax.dot_general(a, b, (((1,), (1,)), ((), ())), preferred_element_type=F32)


def _dot_tn(a, b):
    return lax.dot_general(a, b, (((0,), (0,)), ((), ())), preferred_element_type=F32)


def _dot_hi(a, b):
    return jnp.dot(a, b, preferred_element_type=F32, precision=HI)


def _ada_kernel(c_ref, w_ref, b_ref, o_ref):
    s = _silu(c_ref[...]).astype(BF16)
    o_ref[0] = _dot(s, w_ref[0].astype(BF16)) + b_ref[0]


def _ada(c_all, w_ada, b_ada):
    depth, d, n = w_ada.shape
    r = c_all.shape[0]
    tn = 1024
    return pl.pallas_call(
        _ada_kernel,
        grid=(depth, n // tn),
        in_specs=[pl.BlockSpec((r, d), lambda l, j: (0, 0)),
                  pl.BlockSpec((1, d, tn), lambda l, j: (l, 0, j)),
                  pl.BlockSpec((1, 1, tn), lambda l, j: (l, 0, j))],
        out_specs=pl.BlockSpec((1, r, tn), lambda l, j: (l, 0, j)),
        out_shape=jax.ShapeDtypeStruct((depth, r, n), F32),
        compiler_params=_params(("arbitrary", "arbitrary")),
        name="ada",
    )(c_all, w_ada, b_ada.reshape(depth, 1, n))


def _mod_spec(mod, t, tm, row_axis):
    ngrp, r, d = mod.shape
    if r == 1:
        tiles_per_grp = (t // ngrp) // tm
        return pl.BlockSpec((1, 1, d), lambda *ix: (ix[row_axis] // tiles_per_grp, 0, 0))
    return pl.BlockSpec((1, tm, d), lambda *ix: (0, ix[row_axis], 0))


def _inproj_kernel(x_ref, g_ref, sc_ref, sh_ref, w_ref, o_ref):
    h = (_rms(x_ref[...]) * g_ref[...]) * (1.0 + sc_ref[0]) + sh_ref[0]
    o_ref[...] = _dot(h.astype(BF16), w_ref[...])


def _inproj(x, g, sc, sh, w, tm, tn):
    t, d = x.shape
    n = w.shape[1]
    mod_spec = _mod_spec(sc, t, tm, 1)
    return pl.pallas_call(
        _inproj_kernel,
        grid=(n // tn, t // tm),
        in_specs=[pl.BlockSpec((tm, d), lambda j, i: (i, 0)),
                  pl.BlockSpec((1, d), lambda j, i: (0, 0)),
                  mod_spec, mod_spec,
                  pl.BlockSpec((d, tn), lambda j, i: (0, j))],
        out_specs=pl.BlockSpec((tm, tn), lambda j, i: (i, j)),
        out_shape=jax.ShapeDtypeStruct((t, n), F32),
        compiler_params=_params(("arbitrary", "arbitrary")),
        name="inproj",
    )(x, g, sc, sh, w)


def _conv_kernel(a_ref, g_ref, buf_ref, dw_ref, db_ref, lg_ref, lb_ref, y_ref, st_ref, full_ref,
                 *, lc, nchunks):
    halo = CONV_W + 1
    j = pl.program_id(1)

    @pl.when(j == 0)
    def _():
        full_ref[0:2, :] = jnp.zeros((2, full_ref.shape[1]), F32)
        full_ref[2:halo, :] = buf_ref[0]

    u = a_ref[...] * jax.nn.sigmoid(g_ref[...])
    full_ref[halo:halo + lc, :] = u
    acc = jnp.zeros_like(u)
    for w in range(CONV_W):
        acc = acc + full_ref[2 + w:2 + w + lc, :] * dw_ref[w:w + 1, :]
    acc = acc + db_ref[...]
    mu = jnp.mean(acc, axis=-1, keepdims=True)
    cen = acc - mu
    var = jnp.mean(cen * cen, axis=-1, keepdims=True)
    y = cen * lax.rsqrt(var + NORM_EPS) * lg_ref[...] + lb_ref[...]
    y_ref[...] = _silu(y)

    @pl.when(j == nchunks - 1)
    def _():
        st_ref[0] = full_ref[2 + lc:halo + lc, :]

    if nchunks > 1:
        full_ref[0:halo, :] = full_ref[lc:lc + halo, :]


def _conv(u, buf, dw, db, lg, lb, nb, seq, lc):
    c = buf.shape[2]
    nchunks = seq // lc
    vec = pl.BlockSpec((1, c), lambda b, j: (0, 0))
    return pl.pallas_call(
        functools.partial(_conv_kernel, lc=lc, nchunks=nchunks),
        grid=(nb, nchunks),
        in_specs=[pl.BlockSpec((lc, c), lambda b, j: (b * nchunks + j, 0)),
                  pl.BlockSpec((lc, c), lambda b, j: (b * nchunks + j, 1)),
                  pl.BlockSpec((1, CONV_W - 1, c), lambda b, j: (b, 0, 0)),
                  pl.BlockSpec((CONV_W, c), lambda b, j: (0, 0)),
                  vec, vec, vec],
        out_specs=[pl.BlockSpec((lc, c), lambda b, j: (b * nchunks + j, 0)),
                   pl.BlockSpec((1, CONV_W - 1, c), lambda b, j: (b, 0, 0))],
        out_shape=[jax.ShapeDtypeStruct((nb * seq, c), F32),
                   jax.ShapeDtypeStruct((nb, CONV_W - 1, c), F32)],
        scratch_shapes=[pltpu.VMEM((CONV_W + 1 + lc, c), F32)],
        compiler_params=_params(("arbitrary", "arbitrary")),
        name="conv",
    )(u, u, buf, dw, db, lg, lb)


def _mlstm_kernel(q_ref, k_ref, v_ref, o_ref, gc_ref, gr_ref, bc_ref, br_ref, c0_ref, n0_ref, m0_ref,
                  y_ref, cout_ref, nout_ref, mout_ref, c_s, n_s, m_s, *, c, nc, nh):
    j = pl.program_id(1)
    scale = HEAD_DIM ** -0.5
    mm = BF16 if c % 16 == 0 else F32

    @pl.when(j == 0)
    def _():
        c_s[...] = c0_ref[0]
        n_s[...] = n0_ref[0]
        for h in range(nh):
            m_s[h] = jnp.broadcast_to(m0_ref[0][:, h:h + 1], m_s.shape[1:])

    gcol = gc_ref[...] + bc_ref[...]
    grow = gr_ref[0, 0] + br_ref[...]
    lf_col = _log_sigmoid(gcol)
    lf_row = _log_sigmoid(grow)
    row = lax.broadcasted_iota(jnp.int32, (c, c), 0)
    col = lax.broadcasted_iota(jnp.int32, (c, c), 1)
    causal = col <= row
    b_col_all = _dot_hi(causal.astype(F32), lf_col)
    b_row_all = _dot_hi(lf_row, (row <= col).astype(F32))

    for h in range(nh):
        sl = slice(h * HEAD_DIM, (h + 1) * HEAD_DIM)
        q = q_ref[:, sl]
        k = k_ref[:, sl] * scale
        v = v_ref[:, sl]
        b_c = b_col_all[:, nh + h:nh + h + 1]
        b_r = b_row_all[nh + h:nh + h + 1, :]
        ig_r = grow[h:h + 1, :]
        ig_c = gcol[:, h:h + 1]
        m_prev = m_s[h][0:1, 0:1]
        dmat = jnp.where(causal, b_c - b_r + ig_r, -jnp.inf)
        inter = b_c + m_prev
        m_t = jnp.maximum(inter, jnp.max(dmat, axis=1, keepdims=True))
        dw = jnp.exp(dmat - m_t)
        iw = jnp.exp(inter - m_t)
        qm, km, vm = q.astype(mm), k.astype(mm), v.astype(mm)
        s = _dot_nt(qm, km) * dw
        c_h = c_s[h]
        n_h = n_s[h:h + 1, :]
        num = _dot(s.astype(mm), vm) + iw * _dot(qm, c_h.astype(mm))
        den = jnp.sum(s, axis=1, keepdims=True) + iw * jnp.sum(q * n_h, axis=1, keepdims=True)
        hid = num / jnp.maximum(jnp.abs(den), jnp.exp(-m_t))
        y_ref[:, sl] = jax.nn.sigmoid(o_ref[:, sl]) * hid
        m_last = m_t[c - 1:c, :]
        w_end = jnp.exp(b_c[c - 1:c, :] - b_c + ig_c - m_last)
        f_end = iw[c - 1:c, :]
        kw = k * w_end
        c_s[h] = f_end * c_h + _dot_tn(kw.astype(mm), vm)
        n_s[h:h + 1, :] = f_end * n_h + jnp.sum(kw, axis=0, keepdims=True)
        m_s[h] = jnp.broadcast_to(m_last, m_s.shape[1:])

    @pl.when(j == nc - 1)
    def _():
        cout_ref[0] = c_s[...]
        nout_ref[0] = n_s[...]
        lane = lax.broadcasted_iota(jnp.int32, (1, nh), 1)
        m_out = jnp.zeros((1, nh), F32)
        for h in range(nh):
            m_out = jnp.where(lane == h, m_s[h][0:1, 0:1], m_out)
        mout_ref[0] = m_out


def _mlstm(u, gates, gates_t, bias_row, bias_col, c0, n0, m0, nb, seq, col0):
    nh = c0.shape[1]
    w = nh * HEAD_DIM
    c = MLSTM_CHUNK if seq % MLSTM_CHUNK == 0 else seq
    nc = seq // c

    def ublk(off):
        return pl.BlockSpec((c, w), lambda b, j: (b * nc + j, col0 + off))

    return pl.pallas_call(
        functools.partial(_mlstm_kernel, c=c, nc=nc, nh=nh),
        grid=(nb, nc),
        in_specs=[ublk(0), ublk(1), ublk(2), ublk(3),
                  pl.BlockSpec((c, GATE_COLS), lambda b, j: (b * nc + j, 0)),
                  pl.BlockSpec((1, 1, 2 * nh, c), lambda b, j: (b, j, 0, 0)),
                  pl.BlockSpec((1, GATE_COLS), lambda b, j: (0, 0)),
                  pl.BlockSpec((2 * nh, 1), lambda b, j: (0, 0)),
                  pl.BlockSpec((1, nh, HEAD_DIM, HEAD_DIM), lambda b, j: (b, 0, 0, 0)),
                  pl.BlockSpec((1, nh, HEAD_DIM), lambda b, j: (b, 0, 0)),
                  pl.BlockSpec((1, 1, nh), lambda b, j: (b, 0, 0))],
        out_specs=[pl.BlockSpec((c, w), lambda b, j: (b * nc + j, 0)),
                   pl.BlockSpec((1, nh, HEAD_DIM, HEAD_DIM), lambda b, j: (b, 0, 0, 0)),
                   pl.BlockSpec((1, nh, HEAD_DIM), lambda b, j: (b, 0, 0)),
                   pl.BlockSpec((1, 1, nh), lambda b, j: (b, 0, 0))],
        out_shape=[jax.ShapeDtypeStruct((nb * seq, w), F32),
                   jax.ShapeDtypeStruct((nb, nh, HEAD_DIM, HEAD_DIM), F32),
                   jax.ShapeDtypeStruct((nb, nh, HEAD_DIM), F32),
                   jax.ShapeDtypeStruct((nb, 1, nh), F32)],
        scratch_shapes=[pltpu.VMEM((nh, HEAD_DIM, HEAD_DIM), F32),
                        pltpu.VMEM((nh, HEAD_DIM), F32),
                        pltpu.VMEM((nh, 8, LANES), F32)],
        compiler_params=_params(("arbitrary", "arbitrary")),
        name="mlstm",
    )(u, u, u, u, gates, gates_t, bias_row, bias_col, c0, n0, m0.reshape(nb, 1, nh))


def _fox_prompt_kernel(q_ref, k_ref, v_ref, fc_ref, fr_ref, o_ref, *, seq, tq):
    scale = HEAD_DIM ** -0.5
    kb = k_ref[...].astype(BF16)
    vb = v_ref[...].astype(BF16)
    f_row = fr_ref[0, 0]
    for qi in range(seq // tq):
        rows = slice(qi * tq, (qi + 1) * tq)
        kend = (qi + 1) * tq
        q = (q_ref[rows, :] * scale).astype(BF16)
        s = _dot_nt(q, kb[:kend]) + fc_ref[0, 0][rows, :] - f_row[:, :kend]
        qpos = lax.broadcasted_iota(jnp.int32, (tq, kend), 0) + qi * tq
        kpos = lax.broadcasted_iota(jnp.int32, (tq, kend), 1)
        s = jnp.where(kpos <= qpos, s, -jnp.inf)
        p = jnp.exp(s - jnp.max(s, axis=1, keepdims=True))
        denom = jnp.sum(p, axis=1, keepdims=True)
        o_ref[rows, :] = _dot(p.astype(BF16), vb[:kend]) / denom


def _fox_prompt(u, f_col, f_row, nb, seq, nh, qcol, kcol, vcol):
    tq = min(256, seq)
    return pl.pallas_call(
        functools.partial(_fox_prompt_kernel, seq=seq, tq=tq),
        grid=(nb, nh),
        in_specs=[pl.BlockSpec((seq, HEAD_DIM), lambda b, h: (b, qcol + h)),
                  pl.BlockSpec((seq, HEAD_DIM), lambda b, h: (b, kcol + h)),
                  pl.BlockSpec((seq, HEAD_DIM), lambda b, h: (b, vcol + h)),
                  pl.BlockSpec((1, 1, seq, 1), lambda b, h: (b, h, 0, 0)),
                  pl.BlockSpec((1, 1, 1, seq), lambda b, h: (b, h, 0, 0))],
        out_specs=pl.BlockSpec((seq, HEAD_DIM), lambda b, h: (b, h)),
        out_shape=jax.ShapeDtypeStruct((nb * seq, nh * HEAD_DIM), F32),
        compiler_params=_params(("arbitrary", "arbitrary")),
        name="fox_prompt",
    )(u, u, u, f_col, f_row)


def _fox_sample_kernel(pt_ref, q_ref, kn_ref, vn_ref, lfn_ref, lfnt_ref, *rest, npg, ngrp, nh, s_new):
    k_refs = rest[0:npg]
    v_refs = rest[npg:2 * npg]
    lf_refs = rest[2 * npg:3 * npg]
    o_ref, m_s, l_s, acc_s, carry_s = rest[3 * npg:]
    g = pl.program_id(1)
    scale = HEAD_DIM ** -0.5
    page = k_refs[0].shape[2]
    qs = q_ref[...] * scale
    row = lax.broadcasted_iota(jnp.int32, (s_new, s_new), 0)
    col = lax.broadcasted_iota(jnp.int32, (s_new, s_new), 1)
    f_col = _dot_hi((col <= row).astype(F32), lfn_ref[0])

    @pl.when(g == 0)
    def _():
        f_row = _dot_hi(lfnt_ref[0], (row <= col).astype(F32))
        for h in range(nh):
            sl = slice(h * HEAD_DIM, (h + 1) * HEAD_DIM)
            s = _dot_nt(qs[:, sl], kn_ref[:, sl]) + f_col[:, h:h + 1] - f_row[h:h + 1, :]
            s = jnp.where(col <= row, s, -jnp.inf)
            m = jnp.max(s, axis=1, keepdims=True)
            p = jnp.exp(s - m)
            m_s[h] = m
            l_s[h] = jnp.sum(p, axis=1, keepdims=True)
            acc_s[h] = _dot(p, vn_ref[:, sl])
        carry_s[...] = jnp.zeros(carry_s.shape, F32)

    jrow = lax.broadcasted_iota(jnp.int32, (page, page), 0)
    scol = lax.broadcasted_iota(jnp.int32, (page, page), 1)
    later = (jrow > scol).astype(F32)
    carry = carry_s[...]
    biases = [None] * npg
    for r in reversed(range(npg)):
        lft = lf_refs[r][0, 0]
        biases[r] = carry + _dot_hi(lft, later)
        carry = carry + jnp.sum(lft, axis=1, keepdims=True)
    carry_s[...] = carry
    bias = jnp.concatenate(biases, axis=1)

    for h in range(nh):
        sl = slice(h * HEAD_DIM, (h + 1) * HEAD_DIM)
        qh = qs[:, sl].astype(BF16)
        s = jnp.concatenate([_dot_nt(qh, k_refs[r][0, 0][:, sl].astype(BF16)) for r in range(npg)], axis=1)
        s = s + f_col[:, h:h + 1] + bias[h:h + 1, :]
        m_old = m_s[h]
        m_new = jnp.maximum(m_old, jnp.max(s, axis=1, keepdims=True))
        alpha = jnp.exp(m_old - m_new)
        p = jnp.exp(s - m_new)
        l_s[h] = alpha * l_s[h] + jnp.sum(p, axis=1, keepdims=True)
        pv = _dot(p[:, 0:page].astype(BF16), v_refs[0][0, 0][:, sl].astype(BF16))
        for r in range(1, npg):
            pv = pv + _dot(p[:, r * page:(r + 1) * page].astype(BF16), v_refs[r][0, 0][:, sl].astype(BF16))
        acc_s[h] = alpha * acc_s[h] + pv
        m_s[h] = m_new

    @pl.when(g == ngrp - 1)
    def _():
        for h in range(nh):
            sl = slice(h * HEAD_DIM, (h + 1) * HEAD_DIM)
            o_ref[:, sl] = acc_s[h] / l_s[h]


def _fox_sample(u, lfn, lfn_t, cache_k4, cache_v4, cache_lft, page_table, layer, nb, s_new, nh, qblk):
    w = nh * HEAD_DIM
    page = cache_k4.shape[2]
    n_pages = page_table.shape[1]
    npg = 8 if n_pages % 8 == 0 else 1
    ngrp = n_pages // npg

    def page_spec(shape, r):
        return pl.BlockSpec((1, 1) + shape,
                            lambda b, g, pt: (layer, pt[b, (ngrp - 1 - g) * npg + r], 0, 0))

    kv_specs = [page_spec((page, w), r) for r in range(npg)]
    lf_specs = [page_spec((nh, page), r) for r in range(npg)]
    grid_spec = pltpu.PrefetchScalarGridSpec(
        num_scalar_prefetch=1,
        grid=(nb, ngrp),
        in_specs=[pl.BlockSpec((s_new, w), lambda b, g, pt: (b, qblk)),
                  pl.BlockSpec((s_new, w), lambda b, g, pt: (b, qblk + 1)),
                  pl.BlockSpec((s_new, w), lambda b, g, pt: (b, qblk + 2)),
                  pl.BlockSpec((1, s_new, nh), lambda b, g, pt: (b, 0, 0)),
                  pl.BlockSpec((1, nh, s_new), lambda b, g, pt: (b, 0, 0))] + kv_specs + kv_specs + lf_specs,
        out_specs=pl.BlockSpec((s_new, w), lambda b, g, pt: (b, 0)),
        scratch_shapes=[pltpu.VMEM((nh, s_new, 1), F32),
                        pltpu.VMEM((nh, s_new, 1), F32),
                        pltpu.VMEM((nh, s_new, HEAD_DIM), F32),
                        pltpu.VMEM((nh, 1), F32)],
    )
    return pl.pallas_call(
        functools.partial(_fox_sample_kernel, npg=npg, ngrp=ngrp, nh=nh, s_new=s_new),
        grid_spec=grid_spec,
        out_shape=jax.ShapeDtypeStruct((nb * s_new, w), F32),
        compiler_params=_params(("arbitrary", "arbitrary")),
        name="fox_sample",
    )(page_table, u, u, u, lfn, lfn_t, *([cache_k4] * npg), *([cache_v4] * npg), *([cache_lft] * npg))


def _outproj_kernel(yc_ref, ym_ref, ya_ref, x_ref, wc_ref, wm_ref, wa_ref, gpost_ref, gt_ref, gpre_ref,
                    sc_ref, sh_ref, wr_ref, x1_ref, h2_ref, sco_ref):
    mix = (_dot(yc_ref[...].astype(BF16), wc_ref[...]) + _dot(ym_ref[...].astype(BF16), wm_ref[...])
           + _dot(ya_ref[...].astype(BF16), wa_ref[...]))
    x1 = x_ref[...] + gt_ref[0] * (_rms(mix) * gpost_ref[...])
    x1_ref[...] = x1
    h2 = (_rms(x1) * gpre_ref[...]) * (1.0 + sc_ref[0]) + sh_ref[0]
    h2_ref[...] = h2
    sco_ref[...] = jax.nn.sigmoid(_dot_hi(h2, wr_ref[...]))


def _outproj(yc, ym, ya, x, w_out, gpost, gt, gpre, sc, sh, w_router, tm):
    t, d = x.shape
    cw, mw, aw = yc.shape[1], ym.shape[1], ya.shape[1]
    ne = w_router.shape[1]
    mod_spec = _mod_spec(sc, t, tm, 0)
    vec = pl.BlockSpec((1, d), lambda i: (0, 0))
    return pl.pallas_call(
        _outproj_kernel,
        grid=(t // tm,),
        in_specs=[pl.BlockSpec((tm, cw), lambda i: (i, 0)),
                  pl.BlockSpec((tm, mw), lambda i: (i, 0)),
                  pl.BlockSpec((tm, aw), lambda i: (i, 0)),
                  pl.BlockSpec((tm, d), lambda i: (i, 0)),
                  pl.BlockSpec((cw, d), lambda i: (0, 0)),
                  pl.BlockSpec((mw, d), lambda i: (cw // mw, 0)),
                  pl.BlockSpec((aw, d), lambda i: ((cw + mw) // aw, 0)),
                  vec, mod_spec, vec, mod_spec, mod_spec,
                  pl.BlockSpec((d, ne), lambda i: (0, 0))],
        out_specs=[pl.BlockSpec((tm, d), lambda i: (i, 0)),
                   pl.BlockSpec((tm, d), lambda i: (i, 0)),
                   pl.BlockSpec((tm, ne), lambda i: (i, 0))],
        out_shape=[jax.ShapeDtypeStruct((t, d), F32),
                   jax.ShapeDtypeStruct((t, d), F32),
                   jax.ShapeDtypeStruct((t, ne), F32)],
        compiler_params=_params(("arbitrary",)),
        name="outproj",
    )(yc, ym, ya, x, w_out, w_out, w_out, gpost, gt, gpre, sc, sh, w_router)


def _gather_kernel(idx_ref, src_ref, dst_ref, sem, *, rows):
    base = pl.program_id(0) * rows

    def copy(r):
        return pltpu.make_async_copy(src_ref.at[pl.ds(idx_ref[0, 0, r], 1)],
                                     dst_ref.at[pl.ds(base + r, 1)], sem)

    def start(r, carry):
        copy(r).start()
        return carry

    def wait(r, carry):
        copy(r).wait()
        return carry

    lax.fori_loop(0, rows, start, 0)
    lax.fori_loop(0, rows, wait, 0)


def _gather_rows(src, row_idx, rows):
    p = row_idx.shape[0]
    return pl.pallas_call(
        functools.partial(_gather_kernel, rows=rows),
        grid=(p // rows,),
        in_specs=[pl.BlockSpec((1, 1, rows), lambda i: (i, 0, 0), memory_space=pltpu.SMEM),
                  pl.BlockSpec(memory_space=pl.ANY)],
        out_specs=pl.BlockSpec(memory_space=pl.ANY),
        out_shape=jax.ShapeDtypeStruct((p, src.shape[1]), src.dtype),
        scratch_shapes=[pltpu.SemaphoreType.DMA(())],
        compiler_params=_params(("arbitrary",)),
        name="moe_gather",
    )(row_idx.reshape(p // rows, 1, rows), src)


def _experts_kernel(be_ref, nact_ref, x_ref, wg_ref, wu_ref, wd_ref, y_ref, wg_s, wu_s, wd_s):
    i = pl.program_id(0)
    active = i < nact_ref[0]
    prev = be_ref[jnp.maximum(i - 1, 0)]
    fresh = jnp.logical_or(i == 0, be_ref[i] != prev)

    @pl.when(jnp.logical_and(active, fresh))
    def _():
        wg_s[...] = wg_ref[0, 0].astype(BF16)
        wu_s[...] = wu_ref[0, 0].astype(BF16)
        wd_s[...] = wd_ref[0, 0].astype(BF16)

    @pl.when(active)
    def _():
        x = x_ref[...].astype(BF16)
        hid = _silu(_dot(x, wg_s[...])) * _dot(x, wu_s[...])
        y_ref[...] = _dot(hid.astype(BF16), wd_s[...])

    @pl.when(jnp.logical_not(active))
    def _():
        y_ref[...] = jnp.zeros(y_ref.shape, F32)


def _experts(xs, block_e, n_active, w_gate, w_up, w_down, layer):
    p, d = xs.shape
    de = w_gate.shape[3]
    nblk = p // EXPERT_BLOCK

    def row_map(i, be, nact):
        return (jnp.minimum(i, nact[0] - 1), 0)

    grid_spec = pltpu.PrefetchScalarGridSpec(
        num_scalar_prefetch=2,
        grid=(nblk,),
        in_specs=[pl.BlockSpec((EXPERT_BLOCK, d), row_map),
                  pl.BlockSpec((1, 1, d, de), lambda i, be, nact: (layer, be[i], 0, 0)),
                  pl.BlockSpec((1, 1, d, de), lambda i, be, nact: (layer, be[i], 0, 0)),
                  pl.BlockSpec((1, 1, de, d), lambda i, be, nact: (layer, be[i], 0, 0))],
        out_specs=pl.BlockSpec((EXPERT_BLOCK, d), lambda i, be, nact: (i, 0)),
        scratch_shapes=[pltpu.VMEM((d, de), BF16), pltpu.VMEM((d, de), BF16), pltpu.VMEM((de, d), BF16)],
    )
    return pl.pallas_call(
        _experts_kernel,
        grid_spec=grid_spec,
        out_shape=jax.ShapeDtypeStruct((p, d), F32),
        compiler_params=_params(("arbitrary",)),
        name="moe_experts",
    )(block_e, n_active, xs, w_gate, w_up, w_down)


def _combine_kernel(pos_ref, ys_ref, x1_ref, h2_ref, gw_ref, wsg_ref, wsu_ref, wsd_ref, gpost_ref, gt_ref,
                    o_ref, rows_s, sem, *, tt, topk):
    def copy(t, k):
        return pltpu.make_async_copy(ys_ref.at[pl.ds(pos_ref[0, 0, t * topk + k], 1)],
                                     rows_s.at[k, pl.ds(t, 1)], sem)

    def start(t, carry):
        for k in range(topk):
            copy(t, k).start()
        return carry

    def wait(t, carry):
        for k in range(topk):
            copy(t, k).wait()
        return carry

    lax.fori_loop(0, tt, start, 0)
    h2 = h2_ref[...].astype(BF16)
    shared = _dot((_silu(_dot(h2, wsg_ref[...])) * _dot(h2, wsu_ref[...])).astype(BF16), wsd_ref[...])
    lax.fori_loop(0, tt, wait, 0)
    gw = gw_ref[...]
    routed = gw[:, 0:1] * rows_s[0]
    for k in range(1, topk):
        routed = routed + gw[:, k:k + 1] * rows_s[k]
    f = shared + routed
    o_ref[...] = x1_ref[...] + gt_ref[0] * (_rms(f) * gpost_ref[...])


def _combine(ys, pos, x1, h2, gw, wsg, wsu, wsd, gpost, gt, tt):
    t, d = x1.shape
    topk = gw.shape[1]
    de = wsg.shape[1]
    return pl.pallas_call(
        functools.partial(_combine_kernel, tt=tt, topk=topk),
        grid=(t // tt,),
        in_specs=[pl.BlockSpec((1, 1, tt * topk), lambda i: (i, 0, 0), memory_space=pltpu.SMEM),
                  pl.BlockSpec(memory_space=pl.ANY),
                  pl.BlockSpec((tt, d), lambda i: (i, 0)),
                  pl.BlockSpec((tt, d), lambda i: (i, 0)),
                  pl.BlockSpec((tt, topk), lambda i: (i, 0)),
                  pl.BlockSpec((d, de), lambda i: (0, 0)),
                  pl.BlockSpec((d, de), lambda i: (0, 0)),
                  pl.BlockSpec((de, d), lambda i: (0, 0)),
                  pl.BlockSpec((1, d), lambda i: (0, 0)),
                  _mod_spec(gt, t, tt, 0)],
        out_specs=pl.BlockSpec((tt, d), lambda i: (i, 0)),
        out_shape=jax.ShapeDtypeStruct((t, d), F32),
        scratch_shapes=[pltpu.VMEM((topk, tt, d), F32), pltpu.SemaphoreType.DMA(())],
        compiler_params=_params(("arbitrary",)),
        name="moe_combine",
    )(pos.reshape(t // tt, 1, tt * topk), ys, x1, h2, gw, wsg, wsu, wsd, gpost, gt)


def _route(scores, router_bias):
    t, ne = scores.shape
    sel = scores + router_bias.astype(F32)
    grp_score = jnp.sum(lax.top_k(sel.reshape(t, N_EXPERT_GROUPS, -1), 2)[0], axis=-1)
    _, top_g = lax.top_k(grp_score, TOPK_GROUPS)
    gmask = jnp.any(top_g[..., None] == jnp.arange(N_EXPERT_GROUPS), axis=1)
    sel = jnp.where(jnp.repeat(gmask, ne // N_EXPERT_GROUPS, axis=1), sel, -jnp.inf)
    _, idx = lax.top_k(sel, TOP_K)
    w = jnp.take_along_axis(scores, idx, axis=1)
    w = w / jnp.sum(w, axis=-1, keepdims=True) * ROUTED_SCALE
    return idx, w


def _dispatch(idx, ne, zero_row):
    t, topk = idx.shape
    a = t * topk
    e_flat = idx.reshape(a)
    order = jnp.argsort(e_flat)
    e_sorted = e_flat[order]
    counts = jnp.bincount(e_flat, length=ne)
    padded = (counts + EXPERT_BLOCK - 1) // EXPERT_BLOCK * EXPERT_BLOCK
    starts = jnp.cumsum(counts) - counts
    pends = jnp.cumsum(padded)
    pstarts = pends - padded
    dest = (pstarts[e_sorted] + jnp.arange(a) - starts[e_sorted]).astype(jnp.int32)
    nblk = -(-a // EXPERT_BLOCK) + ne
    p = nblk * EXPERT_BLOCK
    row_tok = jnp.full((p,), zero_row, jnp.int32).at[dest].set((order // topk).astype(jnp.int32))
    pos = jnp.zeros((a,), jnp.int32).at[order].set(dest)
    n_active = (pends[-1] // EXPERT_BLOCK).astype(jnp.int32)
    blk = jnp.minimum(jnp.arange(nblk, dtype=jnp.int32), n_active - 1)
    block_e = jnp.minimum(jnp.searchsorted(pends, blk * EXPERT_BLOCK, side='right'), ne - 1).astype(jnp.int32)
    return row_tok, pos, block_e, n_active.reshape(1)


def _mod_parts(mod, nb_p, reps):
    parts = jnp.split(mod, 6, axis=-1)
    prompt = [m[:nb_p, None, :] for m in parts]
    sample = [jnp.repeat(m[nb_p:], reps, axis=0)[None] for m in parts]
    return prompt, sample


def kernel(x_prompt, x_sample, cache_k, cache_v, cache_logf, state_conv, state_mlstm_C, state_mlstm_n,
           state_mlstm_m, page_table, c_prompt, c_sample, w_ada, b_ada, g_pre_mix, g_post_mix, g_pre_ffn,
           g_post_ffn, w_in, w_out, conv_dw, conv_db, conv_ln_g, conv_ln_b, mlstm_i_bias, mlstm_f_bias,
           fox_f_bias, w_router, router_bias, w_exp_gate, w_exp_up, w_exp_down, w_sh_gate, w_sh_up, w_sh_down):
    bp, lp, d = x_prompt.shape
    bs, ls, _ = x_sample.shape
    depth = w_ada.shape[0]
    cc = conv_dw.shape[2]
    mh = state_mlstm_C.shape[2]
    mw = mh * HEAD_DIM
    ah = cache_k.shape[3]
    aw = ah * HEAD_DIM
    ne = w_router.shape[2]
    n_phys, page = cache_k.shape[1], cache_k.shape[2]
    tp, ts = bp * lp, bs * ls
    n_main = 2 * cc + 4 * mw
    g0 = n_main
    a0 = g0 + 2 * mh
    f0 = a0 + 3 * aw

    xp = x_prompt.reshape(tp, d)
    xs = x_sample.reshape(ts, d)
    mod_all = _ada(jnp.concatenate([c_prompt, c_sample], axis=0), w_ada, b_ada)
    cache_k4 = cache_k.reshape(depth, n_phys, page, aw)
    cache_v4 = cache_v.reshape(depth, n_phys, page, aw)
    cache_lft = jnp.swapaxes(cache_logf, 2, 3)
    zero_conv = jnp.zeros((bp, CONV_W - 1, cc), F32)
    zero_c = jnp.zeros((bp, mh, HEAD_DIM, HEAD_DIM), F32)
    zero_n = jnp.zeros((bp, mh, HEAD_DIM), F32)
    zero_m = jnp.zeros((bp, mh), F32)
    tm_s = ts

    outs = {k: [] for k in ("kp", "vp", "lfp", "ks", "vs", "lfs", "convp", "convs",
                            "cp", "np", "mp", "cs", "ns", "ms")}
    for l in range(depth):
        (sh1p, sc1p, gt1p, sh2p, sc2p, gt2p), (sh1s, sc1s, gt1s, sh2s, sc2s, gt2s) = _mod_parts(mod_all[l], bp, ls)
        w_main = jnp.concatenate([w_in[l][:, :n_main], w_in[l][:, a0:f0]], axis=1).astype(BF16)
        gate_w = jnp.concatenate([w_in[l][:, g0:a0], w_in[l][:, f0:]], axis=1)
        gate_w = jnp.pad(gate_w, ((0, 0), (0, GATE_COLS - gate_w.shape[1]))).astype(BF16)
        gate_bias = jnp.concatenate([mlstm_i_bias[l], mlstm_f_bias[l], fox_f_bias[l]])
        bias_row = jnp.pad(gate_bias, (0, GATE_COLS - gate_bias.shape[0]))[None, :]
        bias_col = gate_bias[:2 * mh, None]
        w_out_b = w_out[l].astype(BF16)
        wsg, wsu, wsd = w_sh_gate[l].astype(BF16), w_sh_up[l].astype(BF16), w_sh_down[l].astype(BF16)
        gpm, gqm = g_pre_mix[l][None, :], g_post_mix[l][None, :]
        gpf, gqf = g_pre_ffn[l][None, :], g_post_ffn[l][None, :]
        dw, db = conv_dw[l], conv_db[l][None, :]
        lng, lnb = conv_ln_g[l][None, :], conv_ln_b[l][None, :]

        groups = []
        for (x, nb, seq, sc1, sh1, tm, tn) in ((xp, bp, lp, sc1p, sh1p, 512, 1024),
                                                (xs, bs, ls, sc1s, sh1s, tm_s, 1024)):
            u = _inproj(x, gpm, sc1, sh1, w_main, tm, tn)
            gates = _inproj(x, gpm, sc1, sh1, gate_w, tm, GATE_COLS)
            groups.append((u, gates))
        (u_p, gates_p), (u_s, gates_s) = groups

        yc_p, conv_p = _conv(u_p, zero_conv, dw, db, lng, lnb, bp, lp, min(512, lp))
        yc_s, conv_s = _conv(u_s, state_conv[l], dw, db, lng, lnb, bs, ls, ls)

        def gates_t(gates, nb, seq):
            c = MLSTM_CHUNK if seq % MLSTM_CHUNK == 0 else seq
            return gates[:, :2 * mh].reshape(nb, seq // c, c, 2 * mh).transpose(0, 1, 3, 2)

        mcol0 = (2 * cc) // mw
        ym_p, c_p, n_p, m_p = _mlstm(u_p, gates_p, gates_t(gates_p, bp, lp), bias_row, bias_col,
                                     zero_c, zero_n, zero_m, bp, lp, mcol0)
        ym_s, c_s, n_s, m_s = _mlstm(u_s, gates_s, gates_t(gates_s, bs, ls), bias_row, bias_col,
                                     state_mlstm_C[l], state_mlstm_n[l], state_mlstm_m[l], bs, ls, mcol0)

        fb = fox_f_bias[l]
        lf_p = _log_sigmoid(gates_p[:, 2 * mh:2 * mh + ah] + fb).reshape(bp, lp, ah)
        lf_s = _log_sigmoid(gates_s[:, 2 * mh:2 * mh + ah] + fb).reshape(bs, ls, ah)
        fcum = jnp.cumsum(lf_p, axis=1).transpose(0, 2, 1)
        qcol = n_main // HEAD_DIM
        ya_p = _fox_prompt(u_p, fcum[..., None], fcum[:, :, None, :], bp, lp, ah,
                           qcol, qcol + ah, qcol + 2 * ah)
        ya_s = _fox_sample(u_s, lf_s, lf_s.transpose(0, 2, 1), cache_k4, cache_v4, cache_lft, page_table, l,
                           bs, ls, ah, n_main // aw)

        x1_p, h2_p, sco_p = _outproj(yc_p, ym_p, ya_p, xp, w_out_b, gqm, gt1p, gpf, sc2p, sh2p, w_router[l], 256)
        x1_s, h2_s, sco_s = _outproj(yc_s, ym_s, ya_s, xs, w_out_b, gqm, gt1s, gpf, sc2s, sh2s, w_router[l], tm_s)

        t_all = tp + ts
        idx, gw = _route(jnp.concatenate([sco_p, sco_s], axis=0), router_bias[l])
        row_tok, pos, block_e, n_active = _dispatch(idx, ne, t_all)
        h2_all = jnp.concatenate([h2_p, h2_s, jnp.zeros((8, d), F32)], axis=0)
        xsort = _gather_rows(h2_all, row_tok, EXPERT_BLOCK)
        ysort = _experts(xsort, block_e, n_active, w_exp_gate, w_exp_up, w_exp_down, l)
        pos2 = pos.reshape(t_all, TOP_K)
        xp = _combine(ysort, pos2[:tp], x1_p, h2_p, gw[:tp], wsg, wsu, wsd, gqf, gt2p, 128)
        xs = _combine(ysort, pos2[tp:], x1_s, h2_s, gw[tp:], wsg, wsu, wsd, gqf, gt2s, min(128, ts))

        na = n_main
        outs["kp"].append(u_p[:, na + aw:na + 2 * aw].reshape(bp, lp, ah, HEAD_DIM))
        outs["vp"].append(u_p[:, na + 2 * aw:na + 3 * aw].reshape(bp, lp, ah, HEAD_DIM))
        outs["lfp"].append(lf_p)
        outs["ks"].append(u_s[:, na + aw:na + 2 * aw].reshape(bs, ls, ah, HEAD_DIM))
        outs["vs"].append(u_s[:, na + 2 * aw:na + 3 * aw].reshape(bs, ls, ah, HEAD_DIM))
        outs["lfs"].append(lf_s)
        outs["convp"].append(conv_p)
        outs["convs"].append(conv_s)
        outs["cp"].append(c_p)
        outs["np"].append(n_p)
        outs["mp"].append(m_p.reshape(bp, mh))
        outs["cs"].append(c_s)
        outs["ns"].append(n_s)
        outs["ms"].append(m_s.reshape(bs, mh))

    st = {k: jnp.stack(v) for k, v in outs.items()}
    return (xp.reshape(bp, lp, d), xs.reshape(bs, ls, d),
            st["kp"], st["vp"], st["lfp"], st["ks"], st["vs"], st["lfs"],
            st["convp"], st["convs"], st["cp"], st["np"], st["mp"], st["cs"], st["ns"], st["ms"])
```

```python
import functools

import jax
import jax.numpy as jnp
from jax import lax
from jax.experimental import pallas as pl
from jax.experimental.pallas import tpu as pltpu

F32 = jnp.float32
BF16 = jnp.bfloat16
I32 = jnp.int32
HI = lax.Precision.HIGHEST

HEAD_DIM = 128
CONV_W = 31
MLSTM_CHUNK = 64
N_EXPERT_GROUPS = 8
TOPK_GROUPS = 4
TOP_K = 8
ROUTED_SCALE = 2.5
NORM_EPS = 1e-6

LANES = 128
GATE_COLS = 128
EXPERT_BLOCK = 256
VMEM_LIMIT = 56 * 1024 * 1024


def _params(sem):
    return pltpu.CompilerParams(dimension_semantics=sem, vmem_limit_bytes=VMEM_LIMIT)


def _log_sigmoid(x):
    return jnp.minimum(x, 0.0) - jnp.log1p(jnp.exp(-jnp.abs(x)))


def _silu(x):
    return x * jax.nn.sigmoid(x)


def _rms(x):
    return x * lax.rsqrt(jnp.mean(x * x, axis=-1, keepdims=True) + NORM_EPS)


def _dot(a, b):
    return jnp.dot(a, b, preferred_element_type=F32)


def _dot_nt(a, b):
    return lax.dot_general(a, b, (((1,), (1,)), ((), ())), preferred_element_type=F32)


def _dot_tn(a, b):
    return lax.dot_general(a, b, (((0,), (0,)), ((), ())), preferred_element_type=F32)


def _dot_hi(a, b):
    return jnp.dot(a, b, preferred_element_type=F32, precision=HI)


def _split3(a):
    hi = a.astype(BF16)
    rest = a - hi.astype(F32)
    mid = rest.astype(BF16)
    lo = (rest - mid.astype(F32)).astype(BF16)
    return hi, mid, lo


def _dot_ones(a, ones):
    ob = ones.astype(BF16)
    hi, mid, lo = _split3(a)
    return _dot(hi, ob) + _dot(mid, ob) + _dot(lo, ob)


def _ones_dot(ones, a):
    ob = ones.astype(BF16)
    hi, mid, lo = _split3(a)
    return _dot(ob, hi) + _dot(ob, mid) + _dot(ob, lo)


def _lane_tile(t, cap):
    best = LANES
    for k in range(1, cap // LANES + 1):
        if t % (k * LANES) == 0:
            best = k * LANES
    return best


def _ada_kernel(c_ref, w_ref, b_ref, o_ref):
    s = _silu(c_ref[...]).astype(BF16)
    o_ref[0] = _dot(s, w_ref[0].astype(BF16)) + b_ref[0]


def _ada(c_all, w_ada, b_ada):
    depth, d, n = w_ada.shape
    r = c_all.shape[0]
    tn = 1024
    return pl.pallas_call(
        _ada_kernel,
        grid=(depth, n // tn),
        in_specs=[pl.BlockSpec((r, d), lambda l, j: (0, 0)),
                  pl.BlockSpec((1, d, tn), lambda l, j: (l, 0, j)),
                  pl.BlockSpec((1, 1, tn), lambda l, j: (l, 0, j))],
        out_specs=pl.BlockSpec((1, r, tn), lambda l, j: (l, 0, j)),
        out_shape=jax.ShapeDtypeStruct((depth, r, n), F32),
        compiler_params=_params(("arbitrary", "arbitrary")),
        name="ada",
    )(c_all, w_ada, b_ada.reshape(depth, 1, n))


def _mod_spec(mod, t, tm, row_axis):
    ngrp, r, d = mod.shape
    if r == 1:
        tiles_per_grp = (t // ngrp) // tm
        return pl.BlockSpec((1, 1, d), lambda *ix: (ix[row_axis] // tiles_per_grp, 0, 0))
    return pl.BlockSpec((1, tm, d), lambda *ix: (0, ix[row_axis], 0))


def _inproj_kernel(x_ref, g_ref, sc_ref, sh_ref, w_ref, o_ref):
    h = (_rms(x_ref[...]) * g_ref[...]) * (1.0 + sc_ref[0]) + sh_ref[0]
    o_ref[...] = _dot(h.astype(BF16), w_ref[...])


def _inproj(x, g, sc, sh, w, tm, tn):
    t, d = x.shape
    n = w.shape[1]
    mod_spec = _mod_spec(sc, t, tm, 1)
    return pl.pallas_call(
        _inproj_kernel,
        grid=(n // tn, t // tm),
        in_specs=[pl.BlockSpec((tm, d), lambda j, i: (i, 0)),
                  pl.BlockSpec((1, d), lambda j, i: (0, 0)),
                  mod_spec, mod_spec,
                  pl.BlockSpec((d, tn), lambda j, i: (0, j))],
        out_specs=pl.BlockSpec((tm, tn), lambda j, i: (i, j)),
        out_shape=jax.ShapeDtypeStruct((t, n), F32),
        compiler_params=_params(("arbitrary", "arbitrary")),
        name="inproj",
    )(x, g, sc, sh, w)


def _conv_kernel(a_ref, g_ref, buf_ref, dw_ref, db_ref, lg_ref, lb_ref, y_ref, st_ref, full_ref,
                 *, lc, nchunks):
    halo = CONV_W + 1
    j = pl.program_id(1)

    @pl.when(j == 0)
    def _():
        full_ref[0:2, :] = jnp.zeros((2, full_ref.shape[1]), F32)
        full_ref[2:halo, :] = buf_ref[0]

    u = a_ref[...] * jax.nn.sigmoid(g_ref[...])
    full_ref[halo:halo + lc, :] = u
    acc = jnp.zeros_like(u)
    for w in range(CONV_W):
        acc = acc + full_ref[2 + w:2 + w + lc, :] * dw_ref[w:w + 1, :]
    acc = acc + db_ref[...]
    mu = jnp.mean(acc, axis=-1, keepdims=True)
    cen = acc - mu
    var = jnp.mean(cen * cen, axis=-1, keepdims=True)
    y = cen * lax.rsqrt(var + NORM_EPS) * lg_ref[...] + lb_ref[...]
    y_ref[...] = _silu(y)

    @pl.when(j == nchunks - 1)
    def _():
        st_ref[0] = full_ref[2 + lc:halo + lc, :]

    if nchunks > 1:
        full_ref[0:halo, :] = full_ref[lc:lc + halo, :]


def _conv(u, buf, dw, db, lg, lb, nb, seq, lc):
    c = buf.shape[2]
    nchunks = seq // lc
    vec = pl.BlockSpec((1, c), lambda b, j: (0, 0))
    return pl.pallas_call(
        functools.partial(_conv_kernel, lc=lc, nchunks=nchunks),
        grid=(nb, nchunks),
        in_specs=[pl.BlockSpec((lc, c), lambda b, j: (b * nchunks + j, 0)),
                  pl.BlockSpec((lc, c), lambda b, j: (b * nchunks + j, 1)),
                  pl.BlockSpec((1, CONV_W - 1, c), lambda b, j: (b, 0, 0)),
                  pl.BlockSpec((CONV_W, c), lambda b, j: (0, 0)),
                  vec, vec, vec],
        out_specs=[pl.BlockSpec((lc, c), lambda b, j: (b * nchunks + j, 0)),
                   pl.BlockSpec((1, CONV_W - 1, c), lambda b, j: (b, 0, 0))],
        out_shape=[jax.ShapeDtypeStruct((nb * seq, c), F32),
                   jax.ShapeDtypeStruct((nb, CONV_W - 1, c), F32)],
        scratch_shapes=[pltpu.VMEM((CONV_W + 1 + lc, c), F32)],
        compiler_params=_params(("arbitrary", "arbitrary")),
        name="conv",
    )(u, u, buf, dw, db, lg, lb)


def _mlstm_kernel(q_ref, k_ref, v_ref, o_ref, gc_ref, gr_ref, bc_ref, br_ref, c0_ref, n0_ref, m0_ref,
                  y_ref, cout_ref, nout_ref, mout_ref, c_s, n_s, m_s, *, c, nc, nh):
    j = pl.program_id(1)
    scale = HEAD_DIM ** -0.5
    mm = BF16 if c % 16 == 0 else F32

    @pl.when(j == 0)
    def _():
        c_s[...] = c0_ref[0]
        n_s[...] = n0_ref[0]
        for h in range(nh):
            m_s[h] = jnp.broadcast_to(m0_ref[0][:, h:h + 1], m_s.shape[1:])

    gcol = gc_ref[...] + bc_ref[...]
    grow = gr_ref[0, 0] + br_ref[...]
    lf_col = _log_sigmoid(gcol)
    lf_row = _log_sigmoid(grow)
    row = lax.broadcasted_iota(I32, (c, c), 0)
    col = lax.broadcasted_iota(I32, (c, c), 1)
    causal = col <= row
    if mm == BF16:
        b_col_all = _ones_dot(causal, lf_col)
        b_row_all = _dot_ones(lf_row, row <= col)
    else:
        b_col_all = _dot_hi(causal.astype(F32), lf_col)
        b_row_all = _dot_hi(lf_row, (row <= col).astype(F32))

    for h in range(nh):
        sl = slice(h * HEAD_DIM, (h + 1) * HEAD_DIM)
        q = q_ref[:, sl]
        k = k_ref[:, sl] * scale
        v = v_ref[:, sl]
        b_c = b_col_all[:, nh + h:nh + h + 1]
        b_r = b_row_all[nh + h:nh + h + 1, :]
        ig_r = grow[h:h + 1, :]
        ig_c = gcol[:, h:h + 1]
        m_prev = m_s[h][0:1, 0:1]
        dmat = jnp.where(causal, b_c - b_r + ig_r, -jnp.inf)
        inter = b_c + m_prev
        m_t = jnp.maximum(inter, jnp.max(dmat, axis=1, keepdims=True))
        dw = jnp.exp(dmat - m_t)
        iw = jnp.exp(inter - m_t)
        qm, km, vm = q.astype(mm), k.astype(mm), v.astype(mm)
        s = _dot_nt(qm, km) * dw
        c_h = c_s[h]
        n_h = n_s[h:h + 1, :]
        num = _dot(s.astype(mm), vm) + iw * _dot(qm, c_h.astype(mm))
        den = jnp.sum(s, axis=1, keepdims=True) + iw * jnp.sum(q * n_h, axis=1, keepdims=True)
        hid = num / jnp.maximum(jnp.abs(den), jnp.exp(-m_t))
        y_ref[:, sl] = jax.nn.sigmoid(o_ref[:, sl]) * hid
        m_last = m_t[c - 1:c, :]
        w_end = jnp.exp(b_c[c - 1:c, :] - b_c + ig_c - m_last)
        f_end = iw[c - 1:c, :]
        kw = k * w_end
        c_s[h] = f_end * c_h + _dot_tn(kw.astype(mm), vm)
        n_s[h:h + 1, :] = f_end * n_h + jnp.sum(kw, axis=0, keepdims=True)
        m_s[h] = jnp.broadcast_to(m_last, m_s.shape[1:])

    @pl.when(j == nc - 1)
    def _():
        cout_ref[0] = c_s[...]
        nout_ref[0] = n_s[...]
        lane = lax.broadcasted_iota(I32, (1, nh), 1)
        m_out = jnp.zeros((1, nh), F32)
        for h in range(nh):
            m_out = jnp.where(lane == h, m_s[h][0:1, 0:1], m_out)
        mout_ref[0] = m_out


def _mlstm(u, gates, gates_t, bias_row, bias_col, c0, n0, m0, nb, seq, col0):
    nh = c0.shape[1]
    w = nh * HEAD_DIM
    c = MLSTM_CHUNK if seq % MLSTM_CHUNK == 0 else seq
    nc = seq // c

    def ublk(off):
        return pl.BlockSpec((c, w), lambda b, j: (b * nc + j, col0 + off))

    return pl.pallas_call(
        functools.partial(_mlstm_kernel, c=c, nc=nc, nh=nh),
        grid=(nb, nc),
        in_specs=[ublk(0), ublk(1), ublk(2), ublk(3),
                  pl.BlockSpec((c, GATE_COLS), lambda b, j: (b * nc + j, 0)),
                  pl.BlockSpec((1, 1, 2 * nh, c), lambda b, j: (b, j, 0, 0)),
                  pl.BlockSpec((1, GATE_COLS), lambda b, j: (0, 0)),
                  pl.BlockSpec((2 * nh, 1), lambda b, j: (0, 0)),
                  pl.BlockSpec((1, nh, HEAD_DIM, HEAD_DIM), lambda b, j: (b, 0, 0, 0)),
                  pl.BlockSpec((1, nh, HEAD_DIM), lambda b, j: (b, 0, 0)),
                  pl.BlockSpec((1, 1, nh), lambda b, j: (b, 0, 0))],
        out_specs=[pl.BlockSpec((c, w), lambda b, j: (b * nc + j, 0)),
                   pl.BlockSpec((1, nh, HEAD_DIM, HEAD_DIM), lambda b, j: (b, 0, 0, 0)),
                   pl.BlockSpec((1, nh, HEAD_DIM), lambda b, j: (b, 0, 0)),
                   pl.BlockSpec((1, 1, nh), lambda b, j: (b, 0, 0))],
        out_shape=[jax.ShapeDtypeStruct((nb * seq, w), F32),
                   jax.ShapeDtypeStruct((nb, nh, HEAD_DIM, HEAD_DIM), F32),
                   jax.ShapeDtypeStruct((nb, nh, HEAD_DIM), F32),
                   jax.ShapeDtypeStruct((nb, 1, nh), F32)],
        scratch_shapes=[pltpu.VMEM((nh, HEAD_DIM, HEAD_DIM), F32),
                        pltpu.VMEM((nh, HEAD_DIM), F32),
                        pltpu.VMEM((nh, 8, LANES), F32)],
        compiler_params=_params(("arbitrary", "arbitrary")),
        name="mlstm",
    )(u, u, u, u, gates, gates_t, bias_row, bias_col, c0, n0, m0.reshape(nb, 1, nh))


def _fox_prompt_kernel(q_ref, k_ref, v_ref, fc_ref, fr_ref, o_ref, *, seq, tq):
    scale = HEAD_DIM ** -0.5
    kb = k_ref[...].astype(BF16)
    vb = v_ref[...].astype(BF16)
    f_row = fr_ref[0, 0]
    for qi in range(seq // tq):
        rows = slice(qi * tq, (qi + 1) * tq)
        kend = (qi + 1) * tq
        q = (q_ref[rows, :] * scale).astype(BF16)
        s = _dot_nt(q, kb[:kend]) + fc_ref[0, 0][rows, :] - f_row[:, :kend]
        qpos = lax.broadcasted_iota(I32, (tq, kend), 0) + qi * tq
        kpos = lax.broadcasted_iota(I32, (tq, kend), 1)
        s = jnp.where(kpos <= qpos, s, -jnp.inf)
        p = jnp.exp(s - jnp.max(s, axis=1, keepdims=True))
        denom = jnp.sum(p, axis=1, keepdims=True)
        o_ref[rows, :] = _dot(p.astype(BF16), vb[:kend]) / denom


def _fox_prompt(u, f_col, f_row, nb, seq, nh, qcol, kcol, vcol):
    tq = min(256, seq)
    return pl.pallas_call(
        functools.partial(_fox_prompt_kernel, seq=seq, tq=tq),
        grid=(nb, nh),
        in_specs=[pl.BlockSpec((seq, HEAD_DIM), lambda b, h: (b, qcol + h)),
                  pl.BlockSpec((seq, HEAD_DIM), lambda b, h: (b, kcol + h)),
                  pl.BlockSpec((seq, HEAD_DIM), lambda b, h: (b, vcol + h)),
                  pl.BlockSpec((1, 1, seq, 1), lambda b, h: (b, h, 0, 0)),
                  pl.BlockSpec((1, 1, 1, seq), lambda b, h: (b, h, 0, 0))],
        out_specs=pl.BlockSpec((seq, HEAD_DIM), lambda b, h: (b, h)),
        out_shape=jax.ShapeDtypeStruct((nb * seq, nh * HEAD_DIM), F32),
        compiler_params=_params(("arbitrary", "arbitrary")),
        name="fox_prompt",
    )(u, u, u, f_col, f_row)


def _pages_per_step(n_pages):
    return 8 if n_pages % 8 == 0 else 1


def _past_bias_kernel(pt_ref, *rest, npg):
    lf_refs = rest[:npg]
    o_ref, carry_s = rest[npg:]
    page = lf_refs[0].shape[3]

    @pl.when(pl.program_id(1) == 0)
    def _():
        carry_s[...] = jnp.zeros(carry_s.shape, F32)

    later = lax.broadcasted_iota(I32, (page, page), 0) > lax.broadcasted_iota(I32, (page, page), 1)
    carry = carry_s[...]
    for r in reversed(range(npg)):
        lft = lf_refs[r][0, 0]
        o_ref[0, :, r * page:(r + 1) * page] = carry + _dot_ones(lft, later)
        carry = carry + jnp.sum(lft, axis=1, keepdims=True)
    carry_s[...] = carry


def _past_bias(cache_lft, page_table, layer):
    nb, n_pages = page_table.shape
    nh, page = cache_lft.shape[2], cache_lft.shape[3]
    npg = _pages_per_step(n_pages)
    ngrp = n_pages // npg
    specs = [pl.BlockSpec((1, 1, nh, page),
                          lambda b, g, pt, r=r: (layer, pt[b, (ngrp - 1 - g) * npg + r], 0, 0))
             for r in range(npg)]
    grid_spec = pltpu.PrefetchScalarGridSpec(
        num_scalar_prefetch=1,
        grid=(nb, ngrp),
        in_specs=specs,
        out_specs=pl.BlockSpec((1, nh, npg * page), lambda b, g, pt: (b, 0, ngrp - 1 - g)),
        scratch_shapes=[pltpu.VMEM((nh, 1), F32)],
    )
    return pl.pallas_call(
        functools.partial(_past_bias_kernel, npg=npg),
        grid_spec=grid_spec,
        out_shape=jax.ShapeDtypeStruct((nb, nh, n_pages * page), F32),
        compiler_params=_params(("arbitrary", "arbitrary")),
        name="fox_past_bias",
    )(page_table, *([cache_lft] * npg))


def _fox_sample_kernel(pt_ref, q_ref, kn_ref, vn_ref, lfc_ref, lfr_ref, *rest, npg, ngrp, nh):
    k_refs = rest[0:npg]
    v_refs = rest[npg:2 * npg]
    b_refs = rest[2 * npg:3 * npg]
    o_ref, m_s, l_s, acc_s = rest[3 * npg:]
    g = pl.program_id(1)
    scale = HEAD_DIM ** -0.5
    nr = q_ref.shape[0]
    ncol = k_refs[0].shape[2]
    hshift = nh.bit_length() - 1
    qb = (q_ref[...] * scale).astype(BF16)

    ri = lax.broadcasted_iota(I32, (nr, nr), 0)
    ci = lax.broadcasted_iota(I32, (nr, nr), 1)
    same_head = (ri & (nh - 1)) == (ci & (nh - 1))
    col_le_row = jnp.logical_and(same_head, (ci >> hshift) <= (ri >> hshift))
    row_le_col = jnp.logical_and(same_head, (ri >> hshift) <= (ci >> hshift))
    f_col = jnp.sum(jnp.where(col_le_row, lfr_ref[0], 0.0), axis=1, keepdims=True)

    @pl.when(g == 0)
    def _():
        f_row = jnp.sum(jnp.where(row_le_col, lfc_ref[...], 0.0), axis=0, keepdims=True)
        s = _dot_nt(qb, kn_ref[...].astype(BF16)) + f_col - f_row
        s = jnp.where(col_le_row, s, -jnp.inf)
        m = jnp.max(s, axis=1, keepdims=True)
        p = jnp.exp(s - m)
        m_s[...] = m
        l_s[...] = jnp.sum(p, axis=1, keepdims=True)
        acc_s[...] = _dot(p.astype(BF16), vn_ref[...].astype(BF16))

    rr = lax.broadcasted_iota(I32, (nr, ncol), 0)
    cc = lax.broadcasted_iota(I32, (nr, ncol), 1)
    base = f_col + jnp.where((rr & (nh - 1)) == (cc & (nh - 1)), 0.0, -jnp.inf)
    s = jnp.concatenate([_dot_nt(qb, k_refs[r][0, 0].astype(BF16)) + (base + b_refs[r][0, 0])
                         for r in range(npg)], axis=1)
    m_old = m_s[...]
    m_new = jnp.maximum(m_old, jnp.max(s, axis=1, keepdims=True))
    alpha = jnp.exp(m_old - m_new)
    p = jnp.exp(s - m_new)
    l_s[...] = alpha * l_s[...] + jnp.sum(p, axis=1, keepdims=True)
    pv = _dot(p[:, 0:ncol].astype(BF16), v_refs[0][0, 0].astype(BF16))
    for r in range(1, npg):
        pv = pv + _dot(p[:, r * ncol:(r + 1) * ncol].astype(BF16), v_refs[r][0, 0].astype(BF16))
    acc_s[...] = alpha * acc_s[...] + pv
    m_s[...] = m_new

    @pl.when(g == ngrp - 1)
    def _():
        o_ref[...] = acc_s[...] / l_s[...]


def _fox_sample(q2, kn2, vn2, lf_col, lf_row, cache_k2, cache_v2, bias, page_table, layer, nh):
    nb, n_pages = page_table.shape
    nr = q2.shape[0] // nb
    ncol = cache_k2.shape[2]
    npg = _pages_per_step(n_pages)
    ngrp = n_pages // npg
    assert nh & (nh - 1) == 0

    kv_specs = [pl.BlockSpec((1, 1, ncol, HEAD_DIM), lambda b, g, pt, r=r: (layer, pt[b, g * npg + r], 0, 0))
                for r in range(npg)]
    b_specs = [pl.BlockSpec((1, 1, 1, ncol), lambda b, g, pt, r=r: (b, g * npg + r, 0, 0)) for r in range(npg)]
    rows = pl.BlockSpec((nr, HEAD_DIM), lambda b, g, pt: (b, 0))
    grid_spec = pltpu.PrefetchScalarGridSpec(
        num_scalar_prefetch=1,
        grid=(nb, ngrp),
        in_specs=[rows, rows, rows,
                  pl.BlockSpec((nr, 1), lambda b, g, pt: (b, 0)),
                  pl.BlockSpec((1, 1, nr), lambda b, g, pt: (b, 0, 0))] + kv_specs + kv_specs + b_specs,
        out_specs=rows,
        scratch_shapes=[pltpu.VMEM((nr, 1), F32), pltpu.VMEM((nr, 1), F32), pltpu.VMEM((nr, HEAD_DIM), F32)],
    )
    return pl.pallas_call(
        functools.partial(_fox_sample_kernel, npg=npg, ngrp=ngrp, nh=nh),
        grid_spec=grid_spec,
        out_shape=jax.ShapeDtypeStruct(q2.shape, F32),
        compiler_params=_params(("arbitrary", "arbitrary")),
        name="fox_sample",
    )(page_table, q2, kn2, vn2, lf_col, lf_row, *([cache_k2] * npg), *([cache_v2] * npg), *([bias] * npg))


def _outproj_kernel(yc_ref, ym_ref, ya_ref, x_ref, wc_ref, wm_ref, wa_ref, gpost_ref, gt_ref, gpre_ref,
                    sc_ref, sh_ref, wrt_ref, x1_ref, h2_ref, sco_ref):
    mix = (_dot(yc_ref[...].astype(BF16), wc_ref[...]) + _dot(ym_ref[...].astype(BF16), wm_ref[...])
           + _dot(ya_ref[...].astype(BF16), wa_ref[...]))
    x1 = x_ref[...] + gt_ref[0] * (_rms(mix) * gpost_ref[...])
    x1_ref[...] = x1
    h2 = (_rms(x1) * gpre_ref[...]) * (1.0 + sc_ref[0]) + sh_ref[0]
    h2_ref[...] = h2
    logits_t = lax.dot_general(wrt_ref[...], h2, (((1,), (1,)), ((), ())),
                               preferred_element_type=F32, precision=HI)
    sco_ref[...] = jax.nn.sigmoid(logits_t)


def _outproj(yc, ym, ya, x, w_out, gpost, gt, gpre, sc, sh, w_router_t, tm):
    t, d = x.shape
    cw, mw, aw = yc.shape[1], ym.shape[1], ya.shape[1]
    ne = w_router_t.shape[0]
    mod_spec = _mod_spec(sc, t, tm, 0)
    vec = pl.BlockSpec((1, d), lambda i: (0, 0))
    return pl.pallas_call(
        _outproj_kernel,
        grid=(t // tm,),
        in_specs=[pl.BlockSpec((tm, cw), lambda i: (i, 0)),
                  pl.BlockSpec((tm, mw), lambda i: (i, 0)),
                  pl.BlockSpec((tm, aw), lambda i: (i, 0)),
                  pl.BlockSpec((tm, d), lambda i: (i, 0)),
                  pl.BlockSpec((cw, d), lambda i: (0, 0)),
                  pl.BlockSpec((mw, d), lambda i: (cw // mw, 0)),
                  pl.BlockSpec((aw, d), lambda i: ((cw + mw) // aw, 0)),
                  vec, mod_spec, vec, mod_spec, mod_spec,
                  pl.BlockSpec((ne, d), lambda i: (0, 0))],
        out_specs=[pl.BlockSpec((tm, d), lambda i: (i, 0)),
                   pl.BlockSpec((tm, d), lambda i: (i, 0)),
                   pl.BlockSpec((ne, tm), lambda i: (0, i))],
        out_shape=[jax.ShapeDtypeStruct((t, d), F32),
                   jax.ShapeDtypeStruct((t, d), F32),
                   jax.ShapeDtypeStruct((ne, t), F32)],
        compiler_params=_params(("arbitrary",)),
        name="outproj",
    )(yc, ym, ya, x, w_out, w_out, w_out, gpost, gt, gpre, sc, sh, w_router_t)


def _route_kernel(s_ref, b_ref, idx_ref, w_ref, rank_ref, cnt_ref, carry_s, *, ne, ng, tt, ntiles):
    i = pl.program_id(0)
    gs = ne // ng

    @pl.when(i == 0)
    def _():
        carry_s[...] = jnp.zeros(carry_s.shape, F32)

    def first_max(x, ids, n):
        m = jnp.max(x, axis=0, keepdims=True)
        return m, jnp.min(jnp.where(x == m, ids, float(n)), axis=0, keepdims=True)

    s = s_ref[...]
    sel = s + b_ref[...]
    sub = lax.broadcasted_iota(I32, (gs, tt), 0).astype(F32)
    grow = lax.broadcasted_iota(I32, (ng, tt), 0).astype(F32)
    gscore = jnp.zeros((ng, tt), F32)
    for g in range(ng):
        blk = sel[g * gs:(g + 1) * gs, :]
        m1, i1 = first_max(blk, sub, gs)
        m2 = jnp.max(jnp.where(sub == i1, -jnp.inf, blk), axis=0, keepdims=True)
        gscore = jnp.where(grow == g, m1 + m2, gscore)
    gsel = jnp.zeros((ng, tt), F32)
    for _ in range(TOPK_GROUPS):
        _, ig = first_max(gscore, grow, ng)
        hit = grow == ig
        gsel = jnp.where(hit, 1.0, gsel)
        gscore = jnp.where(hit, -jnp.inf, gscore)
    cand = jnp.concatenate([jnp.where(gsel[g:g + 1, :] > 0.0, sel[g * gs:(g + 1) * gs, :], -jnp.inf)
                            for g in range(ng)], axis=0)

    erow = lax.broadcasted_iota(I32, (ne, tt), 0).astype(F32)
    member = jnp.zeros((ne, tt), F32)
    picks, weights = [], []
    for _ in range(TOP_K):
        _, ie = first_max(cand, erow, ne)
        hit = erow == ie
        picks.append(ie)
        weights.append(jnp.sum(jnp.where(hit, s, 0.0), axis=0, keepdims=True))
        member = jnp.where(hit, 1.0, member)
        cand = jnp.where(hit, -jnp.inf, cand)
    wsum = weights[0]
    for k in range(1, TOP_K):
        wsum = wsum + weights[k]

    upto = lax.broadcasted_iota(I32, (tt, tt), 0) <= lax.broadcasted_iota(I32, (tt, tt), 1)
    incl = _dot(member.astype(BF16), upto.astype(BF16)) + carry_s[...]
    rank = incl - member
    carry_s[...] = incl[:, tt - 1:tt]

    krow = lax.broadcasted_iota(I32, (TOP_K, tt), 0)
    idx_out = jnp.zeros((TOP_K, tt), F32)
    w_out = jnp.zeros((TOP_K, tt), F32)
    rank_out = jnp.zeros((TOP_K, tt), F32)
    for k in range(TOP_K):
        idx_out = jnp.where(krow == k, picks[k], idx_out)
        w_out = jnp.where(krow == k, weights[k] / wsum * ROUTED_SCALE, w_out)
        rk = jnp.sum(jnp.where(erow == picks[k], rank, 0.0), axis=0, keepdims=True)
        rank_out = jnp.where(krow == k, rk, rank_out)
    idx_ref[...] = idx_out.astype(I32)
    w_ref[...] = w_out
    rank_ref[...] = rank_out

    @pl.when(i == ntiles - 1)
    def _():
        cnt_ref[...] = carry_s[...]


def _route(scores_t, router_bias):
    ne, t = scores_t.shape
    tt = _lane_tile(t, 768)
    ntiles = t // tt
    tok = pl.BlockSpec((TOP_K, tt), lambda i: (0, i))
    return pl.pallas_call(
        functools.partial(_route_kernel, ne=ne, ng=N_EXPERT_GROUPS, tt=tt, ntiles=ntiles),
        grid=(ntiles,),
        in_specs=[pl.BlockSpec((ne, tt), lambda i: (0, i)),
                  pl.BlockSpec((ne, 1), lambda i: (0, 0))],
        out_specs=[tok, tok, tok, pl.BlockSpec((ne, 1), lambda i: (0, 0))],
        out_shape=[jax.ShapeDtypeStruct((TOP_K, t), I32),
                   jax.ShapeDtypeStruct((TOP_K, t), F32),
                   jax.ShapeDtypeStruct((TOP_K, t), F32),
                   jax.ShapeDtypeStruct((ne, 1), F32)],
        scratch_shapes=[pltpu.VMEM((ne, 1), F32)],
        compiler_params=_params(("arbitrary",)),
        name="moe_route",
    )(scores_t, router_bias.astype(F32)[:, None])


def _dispatch(idx_t, rank_t, counts):
    topk, t = idx_t.shape
    ne = counts.shape[0]
    a = t * topk
    counts = counts.reshape(ne).astype(I32)
    padded = (counts + EXPERT_BLOCK - 1) // EXPERT_BLOCK * EXPERT_BLOCK
    pends = jnp.cumsum(padded)
    pstarts = pends - padded
    start_of = jnp.sum(jnp.where(idx_t[:, :, None] == jnp.arange(ne, dtype=I32), pstarts, 0), axis=2)
    pos = (start_of + rank_t.astype(I32)).T
    nblk = -(-a // EXPERT_BLOCK) + ne
    tok = jnp.broadcast_to(jnp.arange(t, dtype=I32)[:, None], (t, topk))
    row_tok = jnp.zeros((nblk * EXPERT_BLOCK,), I32).at[pos.reshape(a)].set(tok.reshape(a), unique_indices=True)
    n_active = pends[-1] // EXPERT_BLOCK
    blk = jnp.minimum(jnp.arange(nblk, dtype=I32), n_active - 1)
    block_e = jnp.minimum(jnp.sum((pends[None, :] <= (blk * EXPERT_BLOCK)[:, None]).astype(I32), axis=1), ne - 1)
    return row_tok, pos, block_e, n_active.reshape(1)


def _experts_kernel(be_ref, nact_ref, idx_ref, idxn_ref, h_ref, wg_ref, wu_ref, wd_ref, y_ref,
                    x_s, wg_s, wu_s, wd_s, sems):
    i = pl.program_id(0)
    nact = nact_ref[0]
    active = i < nact
    slot = lax.rem(i, 2)
    nrows = x_s.shape[1]

    def row_copy(ids_ref, r, sl):
        return pltpu.make_async_copy(h_ref.at[pl.ds(ids_ref[0, 0, r], 1)], x_s.at[sl, pl.ds(r, 1)], sems.at[sl])

    def start_rows(ids_ref, sl):
        def body(r, carry):
            row_copy(ids_ref, r, sl).start()
            return carry
        lax.fori_loop(0, nrows, body, 0)

    @pl.when(jnp.logical_and(i == 0, active))
    def _():
        start_rows(idx_ref, 0)

    @pl.when(i + 1 < nact)
    def _():
        start_rows(idxn_ref, 1 - slot)

    prev = be_ref[jnp.maximum(i - 1, 0)]
    fresh = jnp.logical_or(i == 0, be_ref[i] != prev)

    @pl.when(jnp.logical_and(active, fresh))
    def _():
        wg_s[...] = wg_ref[0, 0].astype(BF16)
        wu_s[...] = wu_ref[0, 0].astype(BF16)
        wd_s[...] = wd_ref[0, 0].astype(BF16)

    @pl.when(active)
    def _():
        def body(r, carry):
            row_copy(idx_ref, r, slot).wait()
            return carry
        lax.fori_loop(0, nrows, body, 0)
        x = x_s[slot].astype(BF16)
        hid = _silu(_dot(x, wg_s[...])) * _dot(x, wu_s[...])
        y_ref[...] = _dot(hid.astype(BF16), wd_s[...])

    @pl.when(jnp.logical_not(active))
    def _():
        y_ref[...] = jnp.zeros(y_ref.shape, F32)


def _experts(h2, row_tok, block_e, n_active, w_gate, w_up, w_down, layer):
    d = h2.shape[1]
    de = w_gate.shape[3]
    nblk = row_tok.shape[0] // EXPERT_BLOCK
    ids = row_tok.reshape(nblk, 1, EXPERT_BLOCK)
    grid_spec = pltpu.PrefetchScalarGridSpec(
        num_scalar_prefetch=2,
        grid=(nblk,),
        in_specs=[pl.BlockSpec((1, 1, EXPERT_BLOCK), lambda i, be, nact: (i, 0, 0), memory_space=pltpu.SMEM),
                  pl.BlockSpec((1, 1, EXPERT_BLOCK), lambda i, be, nact: (jnp.minimum(i + 1, nblk - 1), 0, 0),
                               memory_space=pltpu.SMEM),
                  pl.BlockSpec(memory_space=pl.ANY),
                  pl.BlockSpec((1, 1, d, de), lambda i, be, nact: (layer, be[i], 0, 0)),
                  pl.BlockSpec((1, 1, d, de), lambda i, be, nact: (layer, be[i], 0, 0)),
                  pl.BlockSpec((1, 1, de, d), lambda i, be, nact: (layer, be[i], 0, 0))],
        out_specs=pl.BlockSpec((EXPERT_BLOCK, d), lambda i, be, nact: (i, 0)),
        scratch_shapes=[pltpu.VMEM((2, EXPERT_BLOCK, d), F32),
                        pltpu.VMEM((d, de), BF16), pltpu.VMEM((d, de), BF16), pltpu.VMEM((de, d), BF16),
                        pltpu.SemaphoreType.DMA((2,))],
    )
    return pl.pallas_call(
        _experts_kernel,
        grid_spec=grid_spec,
        out_shape=jax.ShapeDtypeStruct((nblk * EXPERT_BLOCK, d), F32),
        compiler_params=_params(("arbitrary",)),
        name="moe_experts",
    )(block_e, n_active, ids, ids, h2, w_gate, w_up, w_down)


def _combine_kernel(pos_ref, ys_ref, x1_ref, h2_ref, gw_ref, wsg_ref, wsu_ref, wsd_ref, gpost_ref, gt_ref,
                    o_ref, rows_s, sem, *, tt, topk):
    def copy(t, k):
        return pltpu.make_async_copy(ys_ref.at[pl.ds(pos_ref[0, 0, t * topk + k], 1)],
                                     rows_s.at[k, pl.ds(t, 1)], sem)

    def start(t, carry):
        for k in range(topk):
            copy(t, k).start()
        return carry

    def wait(t, carry):
        for k in range(topk):
            copy(t, k).wait()
        return carry

    lax.fori_loop(0, tt, start, 0)
    h2 = h2_ref[...].astype(BF16)
    shared = _dot((_silu(_dot(h2, wsg_ref[...])) * _dot(h2, wsu_ref[...])).astype(BF16), wsd_ref[...])
    lax.fori_loop(0, tt, wait, 0)
    gw = gw_ref[...]
    routed = gw[:, 0:1] * rows_s[0]
    for k in range(1, topk):
        routed = routed + gw[:, k:k + 1] * rows_s[k]
    f = shared + routed
    o_ref[...] = x1_ref[...] + gt_ref[0] * (_rms(f) * gpost_ref[...])


def _combine(ys, pos, x1, h2, gw, wsg, wsu, wsd, gpost, gt, tt):
    t, d = x1.shape
    topk = gw.shape[1]
    de = wsg.shape[1]
    return pl.pallas_call(
        functools.partial(_combine_kernel, tt=tt, topk=topk),
        grid=(t // tt,),
        in_specs=[pl.BlockSpec((1, 1, tt * topk), lambda i: (i, 0, 0), memory_space=pltpu.SMEM),
                  pl.BlockSpec(memory_space=pl.ANY),
                  pl.BlockSpec((tt, d), lambda i: (i, 0)),
                  pl.BlockSpec((tt, d), lambda i: (i, 0)),
                  pl.BlockSpec((tt, topk), lambda i: (i, 0)),
                  pl.BlockSpec((d, de), lambda i: (0, 0)),
                  pl.BlockSpec((d, de), lambda i: (0, 0)),
                  pl.BlockSpec((de, d), lambda i: (0, 0)),
                  pl.BlockSpec((1, d), lambda i: (0, 0)),
                  _mod_spec(gt, t, tt, 0)],
        out_specs=pl.BlockSpec((tt, d), lambda i: (i, 0)),
        out_shape=jax.ShapeDtypeStruct((t, d), F32),
        scratch_shapes=[pltpu.VMEM((topk, tt, d), F32), pltpu.SemaphoreType.DMA(())],
        compiler_params=_params(("arbitrary",)),
        name="moe_combine",
    )(pos.reshape(t // tt, 1, tt * topk), ys, x1, h2, gw, wsg, wsu, wsd, gpost, gt)


def _mod_parts(mod, nb_p, reps):
    parts = jnp.split(mod, 6, axis=-1)
    prompt = [m[:nb_p, None, :] for m in parts]
    sample = [jnp.repeat(m[nb_p:], reps, axis=0)[None] for m in parts]
    return prompt, sample


def kernel(x_prompt, x_sample, cache_k, cache_v, cache_logf, state_conv, state_mlstm_C, state_mlstm_n,
           state_mlstm_m, page_table, c_prompt, c_sample, w_ada, b_ada, g_pre_mix, g_post_mix, g_pre_ffn,
           g_post_ffn, w_in, w_out, conv_dw, conv_db, conv_ln_g, conv_ln_b, mlstm_i_bias, mlstm_f_bias,
           fox_f_bias, w_router, router_bias, w_exp_gate, w_exp_up, w_exp_down, w_sh_gate, w_sh_up, w_sh_down):
    bp, lp, d = x_prompt.shape
    bs, ls, _ = x_sample.shape
    depth = w_ada.shape[0]
    cc = conv_dw.shape[2]
    mh = state_mlstm_C.shape[2]
    mw = mh * HEAD_DIM
    ah = cache_k.shape[3]
    aw = ah * HEAD_DIM
    n_phys, page = cache_k.shape[1], cache_k.shape[2]
    n_pages = page_table.shape[1]
    tp, ts = bp * lp, bs * ls
    n_main = 2 * cc + 4 * mw
    g0 = n_main
    a0 = g0 + 2 * mh
    f0 = a0 + 3 * aw

    xp = x_prompt.reshape(tp, d)
    xs = x_sample.reshape(ts, d)
    mod_all = _ada(jnp.concatenate([c_prompt, c_sample], axis=0), w_ada, b_ada)
    cache_k2 = cache_k.reshape(depth, n_phys, page * ah, HEAD_DIM)
    cache_v2 = cache_v.reshape(depth, n_phys, page * ah, HEAD_DIM)
    cache_lft = jnp.swapaxes(cache_logf, 2, 3)
    zero_conv = jnp.zeros((bp, CONV_W - 1, cc), F32)
    zero_c = jnp.zeros((bp, mh, HEAD_DIM, HEAD_DIM), F32)
    zero_n = jnp.zeros((bp, mh, HEAD_DIM), F32)
    zero_m = jnp.zeros((bp, mh), F32)
    tm_s = ts

    outs = {k: [] for k in ("kp", "vp", "lfp", "ks", "vs", "lfs", "convp", "convs",
                            "cp", "np", "mp", "cs", "ns", "ms")}
    for l in range(depth):
        (sh1p, sc1p, gt1p, sh2p, sc2p, gt2p), (sh1s, sc1s, gt1s, sh2s, sc2s, gt2s) = _mod_parts(mod_all[l], bp, ls)
        w_main = jnp.concatenate([w_in[l][:, :n_main], w_in[l][:, a0:f0]], axis=1).astype(BF16)
        gate_w = jnp.concatenate([w_in[l][:, g0:a0], w_in[l][:, f0:]], axis=1)
        gate_w = jnp.pad(gate_w, ((0, 0), (0, GATE_COLS - gate_w.shape[1]))).astype(BF16)
        gate_bias = jnp.concatenate([mlstm_i_bias[l], mlstm_f_bias[l], fox_f_bias[l]])
        bias_row = jnp.pad(gate_bias, (0, GATE_COLS - gate_bias.shape[0]))[None, :]
        bias_col = gate_bias[:2 * mh, None]
        w_out_b = w_out[l].astype(BF16)
        w_router_t = w_router[l].T
        wsg, wsu, wsd = w_sh_gate[l].astype(BF16), w_sh_up[l].astype(BF16), w_sh_down[l].astype(BF16)
        gpm, gqm = g_pre_mix[l][None, :], g_post_mix[l][None, :]
        gpf, gqf = g_pre_ffn[l][None, :], g_post_ffn[l][None, :]
        dw, db = conv_dw[l], conv_db[l][None, :]
        lng, lnb = conv_ln_g[l][None, :], conv_ln_b[l][None, :]

        groups = []
        for (x, sc1, sh1, tm) in ((xp, sc1p, sh1p, 512), (xs, sc1s, sh1s, tm_s)):
            u = _inproj(x, gpm, sc1, sh1, w_main, tm, 1024)
            gates = _inproj(x, gpm, sc1, sh1, gate_w, tm, GATE_COLS)
            groups.append((u, gates))
        (u_p, gates_p), (u_s, gates_s) = groups

        yc_p, conv_p = _conv(u_p, zero_conv, dw, db, lng, lnb, bp, lp, min(512, lp))
        yc_s, conv_s = _conv(u_s, state_conv[l], dw, db, lng, lnb, bs, ls, ls)

        def gates_t(gates, nb, seq):
            c = MLSTM_CHUNK if seq % MLSTM_CHUNK == 0 else seq
            return gates[:, :2 * mh].reshape(nb, seq // c, c, 2 * mh).transpose(0, 1, 3, 2)

        mcol0 = (2 * cc) // mw
        ym_p, c_p, n_p, m_p = _mlstm(u_p, gates_p, gates_t(gates_p, bp, lp), bias_row, bias_col,
                                     zero_c, zero_n, zero_m, bp, lp, mcol0)
        ym_s, c_s, n_s, m_s = _mlstm(u_s, gates_s, gates_t(gates_s, bs, ls), bias_row, bias_col,
                                     state_mlstm_C[l], state_mlstm_n[l], state_mlstm_m[l], bs, ls, mcol0)

        fb = fox_f_bias[l]
        lf_p = _log_sigmoid(gates_p[:, 2 * mh:2 * mh + ah] + fb).reshape(bp, lp, ah)
        lf_s = _log_sigmoid(gates_s[:, 2 * mh:2 * mh + ah] + fb).reshape(bs, ls, ah)
        fcum = jnp.cumsum(lf_p, axis=1).transpose(0, 2, 1)
        qcol = n_main // HEAD_DIM
        ya_p = _fox_prompt(u_p, fcum[..., None], fcum[:, :, None, :], bp, lp, ah,
                           qcol, qcol + ah, qcol + 2 * ah)
        k_s = u_s[:, n_main + aw:n_main + 2 * aw]
        v_s = u_s[:, n_main + 2 * aw:n_main + 3 * aw]
        past = _past_bias(cache_lft, page_table, l)
        past = past.transpose(0, 2, 1).reshape(bs, n_pages, 1, page * ah)
        ya_s = _fox_sample(u_s[:, n_main:n_main + aw].reshape(ts * ah, HEAD_DIM),
                           k_s.reshape(ts * ah, HEAD_DIM), v_s.reshape(ts * ah, HEAD_DIM),
                           lf_s.reshape(ts * ah, 1), lf_s.reshape(bs, 1, ls * ah),
                           cache_k2, cache_v2, past, page_table, l, ah).reshape(ts, aw)

        x1_p, h2_p, sco_p = _outproj(yc_p, ym_p, ya_p, xp, w_out_b, gqm, gt1p, gpf, sc2p, sh2p, w_router_t, 256)
        x1_s, h2_s, sco_s = _outproj(yc_s, ym_s, ya_s, xs, w_out_b, gqm, gt1s, gpf, sc2s, sh2s, w_router_t, tm_s)

        idx_t, gw_t, rank_t, counts = _route(jnp.concatenate([sco_p, sco_s], axis=1), router_bias[l])
        row_tok, pos, block_e, n_active = _dispatch(idx_t, rank_t, counts)
        h2_all = jnp.concatenate([h2_p, h2_s], axis=0)
        ysort = _experts(h2_all, row_tok, block_e, n_active, w_exp_gate, w_exp_up, w_exp_down, l)
        gw = gw_t.T
        xp = _combine(ysort, pos[:tp], x1_p, h2_p, gw[:tp], wsg, wsu, wsd, gqf, gt2p, 128)
        xs = _combine(ysort, pos[tp:], x1_s, h2_s, gw[tp:], wsg, wsu, wsd, gqf, gt2s, min(128, ts))

        outs["kp"].append(u_p[:, n_main + aw:n_main + 2 * aw].reshape(bp, lp, ah, HEAD_DIM))
        outs["vp"].append(u_p[:, n_main + 2 * aw:n_main + 3 * aw].reshape(bp, lp, ah, HEAD_DIM))
        outs["lfp"].append(lf_p)
        outs["ks"].append(k_s.reshape(bs, ls, ah, HEAD_DIM))
        outs["vs"].append(v_s.reshape(bs, ls, ah, HEAD_DIM))
        outs["lfs"].append(lf_s)
        outs["convp"].append(conv_p)
        outs["convs"].append(conv_s)
        outs["cp"].append(c_p)
        outs["np"].append(n_p)
        outs["mp"].append(m_p.reshape(bp, mh))
        outs["cs"].append(c_s)
        outs["ns"].append(n_s)
        outs["ms"].append(m_s.reshape(bs, mh))

    st = {k: jnp.stack(v) for k, v in outs.items()}
    return (xp.reshape(bp, lp, d), xs.reshape(bs, ls, d),
            st["kp"], st["vp"], st["lfp"], st["ks"], st["vs"], st["lfs"],
            st["convp"], st["convs"], st["cp"], st["np"], st["mp"], st["cs"], st["ns"], st["ms"])
```

```python
import functools

import jax
import jax.numpy as jnp
from jax import lax
from jax.experimental import pallas as pl
from jax.experimental.pallas import tpu as pltpu

F32 = jnp.float32
BF16 = jnp.bfloat16
I32 = jnp.int32
HI = lax.Precision.HIGHEST

HEAD_DIM = 128
CONV_W = 31
MLSTM_CHUNK = 64
N_EXPERT_GROUPS = 8
TOPK_GROUPS = 4
TOP_K = 8
ROUTED_SCALE = 2.5
NORM_EPS = 1e-6

LANES = 128
GATE_COLS = 128
EXPERT_BLOCK = 256
VMEM_LIMIT = 56 * 1024 * 1024


def _params(sem):
    return pltpu.CompilerParams(dimension_semantics=sem, vmem_limit_bytes=VMEM_LIMIT)


def _log_sigmoid(x):
    return jnp.minimum(x, 0.0) - jnp.log1p(jnp.exp(-jnp.abs(x)))


def _silu(x):
    return x * jax.nn.sigmoid(x)


def _rms(x):
    return x * lax.rsqrt(jnp.mean(x * x, axis=-1, keepdims=True) + NORM_EPS)


def _dot(a, b):
    return jnp.dot(a, b, preferred_element_type=F32)


def _dot_nt(a, b):
    return lax.dot_general(a, b, (((1,), (1,)), ((), ())), preferred_element_type=F32)


def _dot_tn(a, b):
    return lax.dot_general(a, b, (((0,), (0,)), ((), ())), preferred_element_type=F32)


def _dot_hi(a, b):
    return jnp.dot(a, b, preferred_element_type=F32, precision=HI)


def _split3(a):
    hi = a.astype(BF16)
    rest = a - hi.astype(F32)
    mid = rest.astype(BF16)
    lo = (rest - mid.astype(F32)).astype(BF16)
    return hi, mid, lo


def _dot_ones(a, ones):
    ob = ones.astype(BF16)
    hi, mid, lo = _split3(a)
    return _dot(hi, ob) + _dot(mid, ob) + _dot(lo, ob)


def _ones_dot(ones, a):
    ob = ones.astype(BF16)
    hi, mid, lo = _split3(a)
    return _dot(ob, hi) + _dot(ob, mid) + _dot(ob, lo)


def _lane_tile(t, cap):
    best = LANES
    for k in range(1, cap // LANES + 1):
        if t % (k * LANES) == 0:
            best = k * LANES
    return best


def _ada_kernel(c_ref, w_ref, b_ref, o_ref):
    s = _silu(c_ref[...]).astype(BF16)
    o_ref[0] = _dot(s, w_ref[0].astype(BF16)) + b_ref[0]


def _ada(c_all, w_ada, b_ada):
    depth, d, n = w_ada.shape
    r = c_all.shape[0]
    tn = 1024
    return pl.pallas_call(
        _ada_kernel,
        grid=(depth, n // tn),
        in_specs=[pl.BlockSpec((r, d), lambda l, j: (0, 0)),
                  pl.BlockSpec((1, d, tn), lambda l, j: (l, 0, j)),
                  pl.BlockSpec((1, 1, tn), lambda l, j: (l, 0, j))],
        out_specs=pl.BlockSpec((1, r, tn), lambda l, j: (l, 0, j)),
        out_shape=jax.ShapeDtypeStruct((depth, r, n), F32),
        compiler_params=_params(("arbitrary", "arbitrary")),
        name="ada",
    )(c_all, w_ada, b_ada.reshape(depth, 1, n))


def _mod_spec(mod, t, tm, row_axis):
    ngrp, r, d = mod.shape
    if r == 1:
        tiles_per_grp = (t // ngrp) // tm
        return pl.BlockSpec((1, 1, d), lambda *ix: (ix[row_axis] // tiles_per_grp, 0, 0))
    return pl.BlockSpec((1, tm, d), lambda *ix: (0, ix[row_axis], 0))


def _inproj_kernel(x_ref, g_ref, sc_ref, sh_ref, w_ref, o_ref):
    h = (_rms(x_ref[...]) * g_ref[...]) * (1.0 + sc_ref[0]) + sh_ref[0]
    o_ref[...] = _dot(h.astype(BF16), w_ref[...])


def _inproj(x, g, sc, sh, w, tm, tn):
    t, d = x.shape
    n = w.shape[1]
    mod_spec = _mod_spec(sc, t, tm, 1)
    return pl.pallas_call(
        _inproj_kernel,
        grid=(n // tn, t // tm),
        in_specs=[pl.BlockSpec((tm, d), lambda j, i: (i, 0)),
                  pl.BlockSpec((1, d), lambda j, i: (0, 0)),
                  mod_spec, mod_spec,
                  pl.BlockSpec((d, tn), lambda j, i: (0, j))],
        out_specs=pl.BlockSpec((tm, tn), lambda j, i: (i, j)),
        out_shape=jax.ShapeDtypeStruct((t, n), F32),
        compiler_params=_params(("arbitrary", "arbitrary")),
        name="inproj",
    )(x, g, sc, sh, w)


def _conv_kernel(a_ref, g_ref, buf_ref, dw_ref, db_ref, lg_ref, lb_ref, y_ref, st_ref, full_ref,
                 *, lc, nchunks):
    halo = CONV_W + 1
    j = pl.program_id(1)

    @pl.when(j == 0)
    def _():
        full_ref[0:2, :] = jnp.zeros((2, full_ref.shape[1]), F32)
        full_ref[2:halo, :] = buf_ref[0]

    u = a_ref[...] * jax.nn.sigmoid(g_ref[...])
    full_ref[halo:halo + lc, :] = u
    acc = jnp.zeros_like(u)
    for w in range(CONV_W):
        acc = acc + full_ref[2 + w:2 + w + lc, :] * dw_ref[w:w + 1, :]
    acc = acc + db_ref[...]
    mu = jnp.mean(acc, axis=-1, keepdims=True)
    cen = acc - mu
    var = jnp.mean(cen * cen, axis=-1, keepdims=True)
    y = cen * lax.rsqrt(var + NORM_EPS) * lg_ref[...] + lb_ref[...]
    y_ref[...] = _silu(y)

    @pl.when(j == nchunks - 1)
    def _():
        st_ref[0] = full_ref[2 + lc:halo + lc, :]

    if nchunks > 1:
        full_ref[0:halo, :] = full_ref[lc:lc + halo, :]


def _conv(u, buf, dw, db, lg, lb, nb, seq, lc):
    c = buf.shape[2]
    nchunks = seq // lc
    vec = pl.BlockSpec((1, c), lambda b, j: (0, 0))
    return pl.pallas_call(
        functools.partial(_conv_kernel, lc=lc, nchunks=nchunks),
        grid=(nb, nchunks),
        in_specs=[pl.BlockSpec((lc, c), lambda b, j: (b * nchunks + j, 0)),
                  pl.BlockSpec((lc, c), lambda b, j: (b * nchunks + j, 1)),
                  pl.BlockSpec((1, CONV_W - 1, c), lambda b, j: (b, 0, 0)),
                  pl.BlockSpec((CONV_W, c), lambda b, j: (0, 0)),
                  vec, vec, vec],
        out_specs=[pl.BlockSpec((lc, c), lambda b, j: (b * nchunks + j, 0)),
                   pl.BlockSpec((1, CONV_W - 1, c), lambda b, j: (b, 0, 0))],
        out_shape=[jax.ShapeDtypeStruct((nb * seq, c), F32),
                   jax.ShapeDtypeStruct((nb, CONV_W - 1, c), F32)],
        scratch_shapes=[pltpu.VMEM((CONV_W + 1 + lc, c), F32)],
        compiler_params=_params(("arbitrary", "arbitrary")),
        name="conv",
    )(u, u, buf, dw, db, lg, lb)


def _mlstm_kernel(q_ref, k_ref, v_ref, o_ref, gc_ref, gr_ref, bc_ref, br_ref, c0_ref, n0_ref, m0_ref,
                  y_ref, cout_ref, nout_ref, mout_ref, c_s, n_s, m_s, *, c, nc, nh, nbs):
    j = pl.program_id(1)

    @pl.when(j == 0)
    def _():
        c_s[...] = c0_ref[...]
        n_s[...] = n0_ref[...]
        for b in range(nbs):
            for h in range(nh):
                m_s[b, h] = jnp.broadcast_to(m0_ref[b][:, h:h + 1], m_s.shape[2:])

    for b in range(nbs):
        _mlstm_chunk(b, q_ref, k_ref, v_ref, o_ref, gc_ref, gr_ref, bc_ref, br_ref, y_ref, c_s, n_s, m_s,
                     c=c, nh=nh)

    @pl.when(j == nc - 1)
    def _():
        cout_ref[...] = c_s[...]
        nout_ref[...] = n_s[...]
        lane = lax.broadcasted_iota(I32, (1, nh), 1)
        for b in range(nbs):
            m_out = jnp.zeros((1, nh), F32)
            for h in range(nh):
                m_out = jnp.where(lane == h, m_s[b, h][0:1, 0:1], m_out)
            mout_ref[b] = m_out


def _mlstm_chunk(b, q_ref, k_ref, v_ref, o_ref, gc_ref, gr_ref, bc_ref, br_ref, y_ref, c_s, n_s, m_s, *, c, nh):
    scale = HEAD_DIM ** -0.5
    mm = BF16 if c % 16 == 0 else F32
    gcol = gc_ref[b] + bc_ref[...]
    grow = gr_ref[b, 0] + br_ref[...]
    lf_col = _log_sigmoid(gcol)
    lf_row = _log_sigmoid(grow)
    row = lax.broadcasted_iota(I32, (c, c), 0)
    col = lax.broadcasted_iota(I32, (c, c), 1)
    causal = col <= row
    if mm == BF16:
        b_col_all = _ones_dot(causal, lf_col)
        b_row_all = _dot_ones(lf_row, row <= col)
    else:
        b_col_all = _dot_hi(causal.astype(F32), lf_col)
        b_row_all = _dot_hi(lf_row, (row <= col).astype(F32))

    for h in range(nh):
        sl = slice(h * HEAD_DIM, (h + 1) * HEAD_DIM)
        q = q_ref[b, :, sl]
        k = k_ref[b, :, sl] * scale
        v = v_ref[b, :, sl]
        b_c = b_col_all[:, nh + h:nh + h + 1]
        b_r = b_row_all[nh + h:nh + h + 1, :]
        ig_r = grow[h:h + 1, :]
        ig_c = gcol[:, h:h + 1]
        m_prev = m_s[b, h][0:1, 0:1]
        dmat = jnp.where(causal, b_c - b_r + ig_r, -jnp.inf)
        inter = b_c + m_prev
        m_t = jnp.maximum(inter, jnp.max(dmat, axis=1, keepdims=True))
        dw = jnp.exp(dmat - m_t)
        iw = jnp.exp(inter - m_t)
        qm, km, vm = q.astype(mm), k.astype(mm), v.astype(mm)
        s = _dot_nt(qm, km) * dw
        c_h = c_s[b, h]
        n_h = n_s[b, h:h + 1, :]
        num = _dot(s.astype(mm), vm) + iw * _dot(qm, c_h.astype(mm))
        den = jnp.sum(s, axis=1, keepdims=True) + iw * jnp.sum(q * n_h, axis=1, keepdims=True)
        hid = num / jnp.maximum(jnp.abs(den), jnp.exp(-m_t))
        y_ref[b, :, sl] = jax.nn.sigmoid(o_ref[b, :, sl]) * hid
        m_last = m_t[c - 1:c, :]
        w_end = jnp.exp(b_c[c - 1:c, :] - b_c + ig_c - m_last)
        f_end = iw[c - 1:c, :]
        kw = k * w_end
        c_s[b, h] = f_end * c_h + _dot_tn(kw.astype(mm), vm)
        n_s[b, h:h + 1, :] = f_end * n_h + jnp.sum(kw, axis=0, keepdims=True)
        m_s[b, h] = jnp.broadcast_to(m_last, m_s.shape[2:])


def _mlstm(u, gates, gates_t, bias_row, bias_col, c0, n0, m0, nb, seq, col0):
    nh = c0.shape[1]
    w = nh * HEAD_DIM
    c = MLSTM_CHUNK if seq % MLSTM_CHUNK == 0 else seq
    nc = seq // c
    nbs = 4 if nb % 4 == 0 else 1
    u3 = u.reshape(nb, seq, u.shape[1])

    def ublk(off):
        return pl.BlockSpec((nbs, c, w), lambda g, j: (g, j, col0 + off))

    y, c_out, n_out, m_out = pl.pallas_call(
        functools.partial(_mlstm_kernel, c=c, nc=nc, nh=nh, nbs=nbs),
        grid=(nb // nbs, nc),
        in_specs=[ublk(0), ublk(1), ublk(2), ublk(3),
                  pl.BlockSpec((nbs, c, GATE_COLS), lambda g, j: (g, j, 0)),
                  pl.BlockSpec((nbs, 1, 2 * nh, c), lambda g, j: (g, j, 0, 0)),
                  pl.BlockSpec((1, GATE_COLS), lambda g, j: (0, 0)),
                  pl.BlockSpec((2 * nh, 1), lambda g, j: (0, 0)),
                  pl.BlockSpec((nbs, nh, HEAD_DIM, HEAD_DIM), lambda g, j: (g, 0, 0, 0)),
                  pl.BlockSpec((nbs, nh, HEAD_DIM), lambda g, j: (g, 0, 0)),
                  pl.BlockSpec((nbs, 1, nh), lambda g, j: (g, 0, 0))],
        out_specs=[pl.BlockSpec((nbs, c, w), lambda g, j: (g, j, 0)),
                   pl.BlockSpec((nbs, nh, HEAD_DIM, HEAD_DIM), lambda g, j: (g, 0, 0, 0)),
                   pl.BlockSpec((nbs, nh, HEAD_DIM), lambda g, j: (g, 0, 0)),
                   pl.BlockSpec((nbs, 1, nh), lambda g, j: (g, 0, 0))],
        out_shape=[jax.ShapeDtypeStruct((nb, seq, w), F32),
                   jax.ShapeDtypeStruct((nb, nh, HEAD_DIM, HEAD_DIM), F32),
                   jax.ShapeDtypeStruct((nb, nh, HEAD_DIM), F32),
                   jax.ShapeDtypeStruct((nb, 1, nh), F32)],
        scratch_shapes=[pltpu.VMEM((nbs, nh, HEAD_DIM, HEAD_DIM), F32),
                        pltpu.VMEM((nbs, nh, HEAD_DIM), F32),
                        pltpu.VMEM((nbs, nh, 8, LANES), F32)],
        compiler_params=_params(("arbitrary", "arbitrary")),
        name="mlstm",
    )(u3, u3, u3, u3, gates.reshape(nb, seq, GATE_COLS), gates_t, bias_row, bias_col, c0, n0,
      m0.reshape(nb, 1, nh))
    return y.reshape(nb * seq, w), c_out, n_out, m_out


def _fox_prompt_kernel(q_ref, k_ref, v_ref, fc_ref, fr_ref, o_ref, *, seq, tq):
    scale = HEAD_DIM ** -0.5
    kb = k_ref[...].astype(BF16)
    vb = v_ref[...].astype(BF16)
    f_row = fr_ref[0, 0]
    for qi in range(seq // tq):
        rows = slice(qi * tq, (qi + 1) * tq)
        kend = (qi + 1) * tq
        q = (q_ref[rows, :] * scale).astype(BF16)
        s = _dot_nt(q, kb[:kend]) + fc_ref[0, 0][rows, :] - f_row[:, :kend]
        qpos = lax.broadcasted_iota(I32, (tq, kend), 0) + qi * tq
        kpos = lax.broadcasted_iota(I32, (tq, kend), 1)
        s = jnp.where(kpos <= qpos, s, -jnp.inf)
        p = jnp.exp(s - jnp.max(s, axis=1, keepdims=True))
        denom = jnp.sum(p, axis=1, keepdims=True)
        o_ref[rows, :] = _dot(p.astype(BF16), vb[:kend]) / denom


def _fox_prompt(u, f_col, f_row, nb, seq, nh, qcol, kcol, vcol):
    tq = min(256, seq)
    return pl.pallas_call(
        functools.partial(_fox_prompt_kernel, seq=seq, tq=tq),
        grid=(nb, nh),
        in_specs=[pl.BlockSpec((seq, HEAD_DIM), lambda b, h: (b, qcol + h)),
                  pl.BlockSpec((seq, HEAD_DIM), lambda b, h: (b, kcol + h)),
                  pl.BlockSpec((seq, HEAD_DIM), lambda b, h: (b, vcol + h)),
                  pl.BlockSpec((1, 1, seq, 1), lambda b, h: (b, h, 0, 0)),
                  pl.BlockSpec((1, 1, 1, seq), lambda b, h: (b, h, 0, 0))],
        out_specs=pl.BlockSpec((seq, HEAD_DIM), lambda b, h: (b, h)),
        out_shape=jax.ShapeDtypeStruct((nb * seq, nh * HEAD_DIM), F32),
        compiler_params=_params(("arbitrary", "arbitrary")),
        name="fox_prompt",
    )(u, u, u, f_col, f_row)


def _pages_per_step(n_pages):
    return 8 if n_pages % 8 == 0 else 1


def _past_bias_kernel(pt_ref, *rest, npg):
    lf_refs = rest[:npg]
    o_ref, carry_s = rest[npg:]
    page = lf_refs[0].shape[3]

    @pl.when(pl.program_id(1) == 0)
    def _():
        carry_s[...] = jnp.zeros(carry_s.shape, F32)

    later = lax.broadcasted_iota(I32, (page, page), 0) > lax.broadcasted_iota(I32, (page, page), 1)
    carry = carry_s[...]
    for r in reversed(range(npg)):
        lft = lf_refs[r][0, 0]
        o_ref[0, :, r * page:(r + 1) * page] = carry + _dot_ones(lft, later)
        carry = carry + jnp.sum(lft, axis=1, keepdims=True)
    carry_s[...] = carry


def _past_bias(cache_lft, page_table, layer):
    nb, n_pages = page_table.shape
    nh, page = cache_lft.shape[2], cache_lft.shape[3]
    npg = 32 if n_pages % 32 == 0 else _pages_per_step(n_pages)
    ngrp = n_pages // npg
    specs = [pl.BlockSpec((1, 1, nh, page),
                          lambda b, g, pt, r=r: (layer, pt[b, (ngrp - 1 - g) * npg + r], 0, 0))
             for r in range(npg)]
    grid_spec = pltpu.PrefetchScalarGridSpec(
        num_scalar_prefetch=1,
        grid=(nb, ngrp),
        in_specs=specs,
        out_specs=pl.BlockSpec((1, nh, npg * page), lambda b, g, pt: (b, 0, ngrp - 1 - g)),
        scratch_shapes=[pltpu.VMEM((nh, 1), F32)],
    )
    return pl.pallas_call(
        functools.partial(_past_bias_kernel, npg=npg),
        grid_spec=grid_spec,
        out_shape=jax.ShapeDtypeStruct((nb, nh, n_pages * page), F32),
        compiler_params=_params(("arbitrary", "arbitrary")),
        name="fox_past_bias",
    )(page_table, *([cache_lft] * npg))


def _fox_sample_kernel(pt_ref, q_ref, kn_ref, vn_ref, lfc_ref, lfr_ref, *rest, npg, ngrp, nh):
    k_refs = rest[0:npg]
    v_refs = rest[npg:2 * npg]
    b_refs = rest[2 * npg:3 * npg]
    o_ref, m_s, l_s, acc_s = rest[3 * npg:]
    g = pl.program_id(1)
    scale = HEAD_DIM ** -0.5
    nr = q_ref.shape[0]
    ncol = k_refs[0].shape[2]
    hshift = nh.bit_length() - 1
    qb = (q_ref[...] * scale).astype(BF16)

    ri = lax.broadcasted_iota(I32, (nr, nr), 0)
    ci = lax.broadcasted_iota(I32, (nr, nr), 1)
    same_head = (ri & (nh - 1)) == (ci & (nh - 1))
    col_le_row = jnp.logical_and(same_head, (ci >> hshift) <= (ri >> hshift))
    row_le_col = jnp.logical_and(same_head, (ri >> hshift) <= (ci >> hshift))
    f_col = jnp.sum(jnp.where(col_le_row, lfr_ref[0], 0.0), axis=1, keepdims=True)

    @pl.when(g == 0)
    def _():
        f_row = jnp.sum(jnp.where(row_le_col, lfc_ref[...], 0.0), axis=0, keepdims=True)
        s = _dot_nt(qb, kn_ref[...].astype(BF16)) + f_col - f_row
        s = jnp.where(col_le_row, s, -jnp.inf)
        m = jnp.max(s, axis=1, keepdims=True)
        p = jnp.exp(s - m)
        m_s[...] = m
        l_s[...] = jnp.sum(p, axis=1, keepdims=True)
        acc_s[...] = _dot(p.astype(BF16), vn_ref[...].astype(BF16))

    rr = lax.broadcasted_iota(I32, (nr, ncol), 0)
    cc = lax.broadcasted_iota(I32, (nr, ncol), 1)
    base = f_col + jnp.where((rr & (nh - 1)) == (cc & (nh - 1)), 0.0, -jnp.inf)
    s = jnp.concatenate([_dot_nt(qb, k_refs[r][0, 0].astype(BF16)) + (base + b_refs[r][0, 0])
                         for r in range(npg)], axis=1)
    m_old = m_s[...]
    m_new = jnp.maximum(m_old, jnp.max(s, axis=1, keepdims=True))
    alpha = jnp.exp(m_old - m_new)
    p = jnp.exp(s - m_new)
    l_s[...] = alpha * l_s[...] + jnp.sum(p, axis=1, keepdims=True)
    pv = _dot(p[:, 0:ncol].astype(BF16), v_refs[0][0, 0].astype(BF16))
    for r in range(1, npg):
        pv = pv + _dot(p[:, r * ncol:(r + 1) * ncol].astype(BF16), v_refs[r][0, 0].astype(BF16))
    acc_s[...] = alpha * acc_s[...] + pv
    m_s[...] = m_new

    @pl.when(g == ngrp - 1)
    def _():
        o_ref[...] = acc_s[...] / l_s[...]


def _fox_sample(q2, kn2, vn2, lf_col, lf_row, cache_k2, cache_v2, bias, page_table, layer, nh):
    nb, n_pages = page_table.shape
    nr = q2.shape[0] // nb
    ncol = cache_k2.shape[2]
    npg = _pages_per_step(n_pages)
    ngrp = n_pages // npg
    assert nh & (nh - 1) == 0

    kv_specs = [pl.BlockSpec((1, 1, ncol, HEAD_DIM), lambda b, g, pt, r=r: (layer, pt[b, g * npg + r], 0, 0))
                for r in range(npg)]
    b_specs = [pl.BlockSpec((1, 1, 1, ncol), lambda b, g, pt, r=r: (b, g * npg + r, 0, 0)) for r in range(npg)]
    rows = pl.BlockSpec((nr, HEAD_DIM), lambda b, g, pt: (b, 0))
    grid_spec = pltpu.PrefetchScalarGridSpec(
        num_scalar_prefetch=1,
        grid=(nb, ngrp),
        in_specs=[rows, rows, rows,
                  pl.BlockSpec((nr, 1), lambda b, g, pt: (b, 0)),
                  pl.BlockSpec((1, 1, nr), lambda b, g, pt: (b, 0, 0))] + kv_specs + kv_specs + b_specs,
        out_specs=rows,
        scratch_shapes=[pltpu.VMEM((nr, 1), F32), pltpu.VMEM((nr, 1), F32), pltpu.VMEM((nr, HEAD_DIM), F32)],
    )
    return pl.pallas_call(
        functools.partial(_fox_sample_kernel, npg=npg, ngrp=ngrp, nh=nh),
        grid_spec=grid_spec,
        out_shape=jax.ShapeDtypeStruct(q2.shape, F32),
        compiler_params=_params(("arbitrary", "arbitrary")),
        name="fox_sample",
    )(page_table, q2, kn2, vn2, lf_col, lf_row, *([cache_k2] * npg), *([cache_v2] * npg), *([bias] * npg))


def _outproj_kernel(yc_ref, ym_ref, ya_ref, x_ref, wc_ref, wm_ref, wa_ref, gpost_ref, gt_ref, gpre_ref,
                    sc_ref, sh_ref, wrt_ref, x1_ref, h2_ref, sco_ref):
    mix = (_dot(yc_ref[...].astype(BF16), wc_ref[...]) + _dot(ym_ref[...].astype(BF16), wm_ref[...])
           + _dot(ya_ref[...].astype(BF16), wa_ref[...]))
    x1 = x_ref[...] + gt_ref[0] * (_rms(mix) * gpost_ref[...])
    x1_ref[...] = x1
    h2 = (_rms(x1) * gpre_ref[...]) * (1.0 + sc_ref[0]) + sh_ref[0]
    h2_ref[...] = h2
    logits_t = lax.dot_general(wrt_ref[...], h2, (((1,), (1,)), ((), ())),
                               preferred_element_type=F32, precision=HI)
    sco_ref[...] = jax.nn.sigmoid(logits_t)


def _outproj(yc, ym, ya, x, w_out, gpost, gt, gpre, sc, sh, w_router_t, tm):
    t, d = x.shape
    cw, mw, aw = yc.shape[1], ym.shape[1], ya.shape[1]
    ne = w_router_t.shape[0]
    mod_spec = _mod_spec(sc, t, tm, 0)
    vec = pl.BlockSpec((1, d), lambda i: (0, 0))
    return pl.pallas_call(
        _outproj_kernel,
        grid=(t // tm,),
        in_specs=[pl.BlockSpec((tm, cw), lambda i: (i, 0)),
                  pl.BlockSpec((tm, mw), lambda i: (i, 0)),
                  pl.BlockSpec((tm, aw), lambda i: (i, 0)),
                  pl.BlockSpec((tm, d), lambda i: (i, 0)),
                  pl.BlockSpec((cw, d), lambda i: (0, 0)),
                  pl.BlockSpec((mw, d), lambda i: (cw // mw, 0)),
                  pl.BlockSpec((aw, d), lambda i: ((cw + mw) // aw, 0)),
                  vec, mod_spec, vec, mod_spec, mod_spec,
                  pl.BlockSpec((ne, d), lambda i: (0, 0))],
        out_specs=[pl.BlockSpec((tm, d), lambda i: (i, 0)),
                   pl.BlockSpec((tm, d), lambda i: (i, 0)),
                   pl.BlockSpec((ne, tm), lambda i: (0, i))],
        out_shape=[jax.ShapeDtypeStruct((t, d), F32),
                   jax.ShapeDtypeStruct((t, d), F32),
                   jax.ShapeDtypeStruct((ne, t), F32)],
        compiler_params=_params(("arbitrary",)),
        name="outproj",
    )(yc, ym, ya, x, w_out, w_out, w_out, gpost, gt, gpre, sc, sh, w_router_t)


def _route_kernel(s_ref, b_ref, idx_ref, w_ref, rank_ref, cnt_ref, carry_s, *, ne, ng, tt, ntiles):
    i = pl.program_id(0)
    gs = ne // ng

    @pl.when(i == 0)
    def _():
        carry_s[...] = jnp.zeros(carry_s.shape, F32)

    def first_max(x, ids, n):
        m = jnp.max(x, axis=0, keepdims=True)
        return m, jnp.min(jnp.where(x == m, ids, float(n)), axis=0, keepdims=True)

    s = s_ref[...]
    sel = s + b_ref[...]
    sub = lax.broadcasted_iota(I32, (gs, tt), 0).astype(F32)
    grow = lax.broadcasted_iota(I32, (ng, tt), 0).astype(F32)
    gscore = jnp.zeros((ng, tt), F32)
    for g in range(ng):
        blk = sel[g * gs:(g + 1) * gs, :]
        m1, i1 = first_max(blk, sub, gs)
        m2 = jnp.max(jnp.where(sub == i1, -jnp.inf, blk), axis=0, keepdims=True)
        gscore = jnp.where(grow == g, m1 + m2, gscore)
    gsel = jnp.zeros((ng, tt), F32)
    for _ in range(TOPK_GROUPS):
        _, ig = first_max(gscore, grow, ng)
        hit = grow == ig
        gsel = jnp.where(hit, 1.0, gsel)
        gscore = jnp.where(hit, -jnp.inf, gscore)
    cand = jnp.concatenate([jnp.where(gsel[g:g + 1, :] > 0.0, sel[g * gs:(g + 1) * gs, :], -jnp.inf)
                            for g in range(ng)], axis=0)

    erow = lax.broadcasted_iota(I32, (ne, tt), 0).astype(F32)
    member = jnp.zeros((ne, tt), F32)
    picks, weights = [], []
    for _ in range(TOP_K):
        _, ie = first_max(cand, erow, ne)
        hit = erow == ie
        picks.append(ie)
        weights.append(jnp.sum(jnp.where(hit, s, 0.0), axis=0, keepdims=True))
        member = jnp.where(hit, 1.0, member)
        cand = jnp.where(hit, -jnp.inf, cand)
    wsum = weights[0]
    for k in range(1, TOP_K):
        wsum = wsum + weights[k]

    upto = lax.broadcasted_iota(I32, (tt, tt), 0) <= lax.broadcasted_iota(I32, (tt, tt), 1)
    incl = _dot(member.astype(BF16), upto.astype(BF16)) + carry_s[...]
    rank = incl - member
    carry_s[...] = incl[:, tt - 1:tt]

    krow = lax.broadcasted_iota(I32, (TOP_K, tt), 0)
    idx_out = jnp.zeros((TOP_K, tt), F32)
    w_out = jnp.zeros((TOP_K, tt), F32)
    rank_out = jnp.zeros((TOP_K, tt), F32)
    for k in range(TOP_K):
        idx_out = jnp.where(krow == k, picks[k], idx_out)
        w_out = jnp.where(krow == k, weights[k] / wsum * ROUTED_SCALE, w_out)
        rk = jnp.sum(jnp.where(erow == picks[k], rank, 0.0), axis=0, keepdims=True)
        rank_out = jnp.where(krow == k, rk, rank_out)
    idx_ref[...] = idx_out.astype(I32)
    w_ref[...] = w_out
    rank_ref[...] = rank_out

    @pl.when(i == ntiles - 1)
    def _():
        cnt_ref[...] = carry_s[...]


def _route(scores_t, router_bias):
    ne, t = scores_t.shape
    tt = _lane_tile(t, 768)
    ntiles = t // tt
    tok = pl.BlockSpec((TOP_K, tt), lambda i: (0, i))
    return pl.pallas_call(
        functools.partial(_route_kernel, ne=ne, ng=N_EXPERT_GROUPS, tt=tt, ntiles=ntiles),
        grid=(ntiles,),
        in_specs=[pl.BlockSpec((ne, tt), lambda i: (0, i)),
                  pl.BlockSpec((ne, 1), lambda i: (0, 0))],
        out_specs=[tok, tok, tok, pl.BlockSpec((ne, 1), lambda i: (0, 0))],
        out_shape=[jax.ShapeDtypeStruct((TOP_K, t), I32),
                   jax.ShapeDtypeStruct((TOP_K, t), F32),
                   jax.ShapeDtypeStruct((TOP_K, t), F32),
                   jax.ShapeDtypeStruct((ne, 1), F32)],
        scratch_shapes=[pltpu.VMEM((ne, 1), F32)],
        compiler_params=_params(("arbitrary",)),
        name="moe_route",
    )(scores_t, router_bias.astype(F32)[:, None])


def _dispatch(idx_t, rank_t, counts):
    topk, t = idx_t.shape
    ne = counts.shape[0]
    a = t * topk
    counts = counts.reshape(ne).astype(I32)
    padded = (counts + EXPERT_BLOCK - 1) // EXPERT_BLOCK * EXPERT_BLOCK
    pends = jnp.cumsum(padded)
    pstarts = pends - padded
    start_of = jnp.sum(jnp.where(idx_t[:, :, None] == jnp.arange(ne, dtype=I32), pstarts, 0), axis=2)
    pos = (start_of + rank_t.astype(I32)).T
    nblk = -(-a // EXPERT_BLOCK) + ne
    tok = jnp.broadcast_to(jnp.arange(t, dtype=I32)[:, None], (t, topk))
    row_tok = jnp.zeros((nblk * EXPERT_BLOCK,), I32).at[pos.reshape(a)].set(tok.reshape(a), unique_indices=True)
    n_active = pends[-1] // EXPERT_BLOCK
    blk = jnp.minimum(jnp.arange(nblk, dtype=I32), n_active - 1)
    block_e = jnp.minimum(jnp.sum((pends[None, :] <= (blk * EXPERT_BLOCK)[:, None]).astype(I32), axis=1), ne - 1)
    return row_tok, pos, block_e, n_active.reshape(1)


def _experts_kernel(be_ref, nact_ref, first_ref, nxt_ref, wsl_ref, idx_ref, idxn_ref, h_ref,
                    wg_hbm, wu_hbm, wd_hbm, y_ref,
                    x_s, wg_f, wu_f, wd_f, wg_s, wu_s, wd_s, xsem, wsem, *, layer):
    i = pl.program_id(0)
    nact = nact_ref[0]
    active = i < nact
    has_next = i + 1 < nact
    slot = lax.rem(i, 2)
    nrows = x_s.shape[1]

    def row_copy(ids_ref, r, sl):
        return pltpu.make_async_copy(h_ref.at[pl.ds(ids_ref[0, 0, r], 1)], x_s.at[sl, pl.ds(r, 1)], xsem.at[sl])

    def wait_rows(sl):
        pltpu.make_async_copy(h_ref.at[pl.ds(0, nrows)], x_s.at[sl], xsem.at[sl]).wait()

    def weight_copies(e, sl):
        return (pltpu.make_async_copy(wg_hbm.at[layer, e], wg_f.at[sl], wsem.at[sl, 0]),
                pltpu.make_async_copy(wu_hbm.at[layer, e], wu_f.at[sl], wsem.at[sl, 1]),
                pltpu.make_async_copy(wd_hbm.at[layer, e], wd_f.at[sl], wsem.at[sl, 2]))

    @pl.when(jnp.logical_and(i == 0, active))
    def _():
        def body(r, carry):
            row_copy(idx_ref, r, 0).start()
            return carry
        lax.fori_loop(0, nrows, body, 0)
        for cp in weight_copies(be_ref[0], 0):
            cp.start()

    @pl.when(jnp.logical_and(active, first_ref[i] == 1))
    def _():
        wsl = wsl_ref[i]

        @pl.when(nxt_ref[i] >= 0)
        def _():
            for cp in weight_copies(nxt_ref[i], 1 - wsl):
                cp.start()

        for cp in weight_copies(be_ref[i], wsl):
            cp.wait()
        wg_s[...] = wg_f[wsl].astype(BF16)
        wu_s[...] = wu_f[wsl].astype(BF16)
        wd_s[...] = wd_f[wsl].astype(BF16)

    def compute(sl, prefetch):
        wait_rows(sl)
        x = x_s[sl].astype(BF16)
        if prefetch:
            for r in range(nrows):
                row_copy(idxn_ref, r, 1 - sl).start()
        hid = _silu(_dot(x, wg_s[...])) * _dot(x, wu_s[...])
        y_ref[...] = _dot(hid.astype(BF16), wd_s[...])

    for sl in (0, 1):
        @pl.when(jnp.logical_and(has_next, slot == sl))
        def _(sl=sl):
            compute(sl, True)

    @pl.when(jnp.logical_and(active, jnp.logical_not(has_next)))
    def _():
        compute(slot, False)

    @pl.when(jnp.logical_not(active))
    def _():
        y_ref[...] = jnp.zeros(y_ref.shape, F32)


def _experts(h2, row_tok, block_e, n_active, w_gate, w_up, w_down, layer):
    d = h2.shape[1]
    de = w_gate.shape[3]
    nblk = row_tok.shape[0] // EXPERT_BLOCK
    ids = row_tok.reshape(nblk, 1, EXPERT_BLOCK)
    blk = jnp.arange(nblk, dtype=I32)
    first = jnp.logical_and(blk < n_active[0],
                            jnp.concatenate([jnp.ones((1,), bool), block_e[1:] != block_e[:-1]]))
    wslot = lax.rem(jnp.cumsum(first.astype(I32)) - 1, 2).astype(I32)
    first_at = jnp.where(first, blk, nblk)
    next_first = lax.cummin(jnp.concatenate([first_at[1:], jnp.full((1,), nblk, I32)]), reverse=True)
    nxt_e = jnp.where(next_first < nblk, block_e[jnp.minimum(next_first, nblk - 1)], -1).astype(I32)
    any_spec = pl.BlockSpec(memory_space=pl.ANY)
    grid_spec = pltpu.PrefetchScalarGridSpec(
        num_scalar_prefetch=5,
        grid=(nblk,),
        in_specs=[pl.BlockSpec((1, 1, EXPERT_BLOCK), lambda i, *_: (i, 0, 0), memory_space=pltpu.SMEM),
                  pl.BlockSpec((1, 1, EXPERT_BLOCK), lambda i, *_: (jnp.minimum(i + 1, nblk - 1), 0, 0),
                               memory_space=pltpu.SMEM),
                  any_spec, any_spec, any_spec, any_spec],
        out_specs=pl.BlockSpec((EXPERT_BLOCK, d), lambda i, *_: (i, 0)),
        scratch_shapes=[pltpu.VMEM((2, EXPERT_BLOCK, d), F32),
                        pltpu.VMEM((2, d, de), F32), pltpu.VMEM((2, d, de), F32), pltpu.VMEM((2, de, d), F32),
                        pltpu.VMEM((d, de), BF16), pltpu.VMEM((d, de), BF16), pltpu.VMEM((de, d), BF16),
                        pltpu.SemaphoreType.DMA((2,)), pltpu.SemaphoreType.DMA((2, 3))],
    )
    return pl.pallas_call(
        functools.partial(_experts_kernel, layer=layer),
        grid_spec=grid_spec,
        out_shape=jax.ShapeDtypeStruct((nblk * EXPERT_BLOCK, d), F32),
        compiler_params=_params(("arbitrary",)),
        name="moe_experts",
    )(block_e, n_active, first.astype(I32), nxt_e, wslot, ids, ids, h2, w_gate, w_up, w_down)


def _combine_kernel(pos_ref, ys_ref, x1_ref, h2_ref, gw_ref, wsg_ref, wsu_ref, wsd_ref, gpost_ref, gt_ref,
                    o_ref, rows_s, sem, *, tt, topk):
    for t in range(tt):
        for k in range(topk):
            pltpu.make_async_copy(ys_ref.at[pl.ds(pos_ref[0, 0, t * topk + k], 1)],
                                  rows_s.at[pl.ds(k * tt + t, 1)], sem).start()
    h2 = h2_ref[...].astype(BF16)
    shared = _dot((_silu(_dot(h2, wsg_ref[...])) * _dot(h2, wsu_ref[...])).astype(BF16), wsd_ref[...])
    pltpu.make_async_copy(ys_ref.at[pl.ds(0, topk * tt)], rows_s, sem).wait()
    gw = gw_ref[...]
    routed = gw[:, 0:1] * rows_s[0:tt, :]
    for k in range(1, topk):
        routed = routed + gw[:, k:k + 1] * rows_s[k * tt:(k + 1) * tt, :]
    f = shared + routed
    o_ref[...] = x1_ref[...] + gt_ref[0] * (_rms(f) * gpost_ref[...])


def _combine(ys, pos, x1, h2, gw, wsg, wsu, wsd, gpost, gt, tt):
    t, d = x1.shape
    topk = gw.shape[1]
    de = wsg.shape[1]
    return pl.pallas_call(
        functools.partial(_combine_kernel, tt=tt, topk=topk),
        grid=(t // tt,),
        in_specs=[pl.BlockSpec((1, 1, tt * topk), lambda i: (i, 0, 0), memory_space=pltpu.SMEM),
                  pl.BlockSpec(memory_space=pl.ANY),
                  pl.BlockSpec((tt, d), lambda i: (i, 0)),
                  pl.BlockSpec((tt, d), lambda i: (i, 0)),
                  pl.BlockSpec((tt, topk), lambda i: (i, 0)),
                  pl.BlockSpec((d, de), lambda i: (0, 0)),
                  pl.BlockSpec((d, de), lambda i: (0, 0)),
                  pl.BlockSpec((de, d), lambda i: (0, 0)),
                  pl.BlockSpec((1, d), lambda i: (0, 0)),
                  _mod_spec(gt, t, tt, 0)],
        out_specs=pl.BlockSpec((tt, d), lambda i: (i, 0)),
        out_shape=jax.ShapeDtypeStruct((t, d), F32),
        scratch_shapes=[pltpu.VMEM((topk * tt, d), F32), pltpu.SemaphoreType.DMA(())],
        compiler_params=_params(("arbitrary",)),
        name="moe_combine",
    )(pos.reshape(t // tt, 1, tt * topk), ys, x1, h2, gw, wsg, wsu, wsd, gpost, gt)


def _mod_parts(mod, nb_p, reps):
    parts = jnp.split(mod, 6, axis=-1)
    prompt = [m[:nb_p, None, :] for m in parts]
    sample = [jnp.repeat(m[nb_p:], reps, axis=0)[None] for m in parts]
    return prompt, sample


def kernel(x_prompt, x_sample, cache_k, cache_v, cache_logf, state_conv, state_mlstm_C, state_mlstm_n,
           state_mlstm_m, page_table, c_prompt, c_sample, w_ada, b_ada, g_pre_mix, g_post_mix, g_pre_ffn,
           g_post_ffn, w_in, w_out, conv_dw, conv_db, conv_ln_g, conv_ln_b, mlstm_i_bias, mlstm_f_bias,
           fox_f_bias, w_router, router_bias, w_exp_gate, w_exp_up, w_exp_down, w_sh_gate, w_sh_up, w_sh_down):
    bp, lp, d = x_prompt.shape
    bs, ls, _ = x_sample.shape
    depth = w_ada.shape[0]
    cc = conv_dw.shape[2]
    mh = state_mlstm_C.shape[2]
    mw = mh * HEAD_DIM
    ah = cache_k.shape[3]
    aw = ah * HEAD_DIM
    n_phys, page = cache_k.shape[1], cache_k.shape[2]
    n_pages = page_table.shape[1]
    tp, ts = bp * lp, bs * ls
    n_main = 2 * cc + 4 * mw
    g0 = n_main
    a0 = g0 + 2 * mh
    f0 = a0 + 3 * aw

    xp = x_prompt.reshape(tp, d)
    xs = x_sample.reshape(ts, d)
    mod_all = _ada(jnp.concatenate([c_prompt, c_sample], axis=0), w_ada, b_ada)
    cache_k2 = cache_k.reshape(depth, n_phys, page * ah, HEAD_DIM)
    cache_v2 = cache_v.reshape(depth, n_phys, page * ah, HEAD_DIM)
    cache_lft = jnp.swapaxes(cache_logf, 2, 3)
    zero_conv = jnp.zeros((bp, CONV_W - 1, cc), F32)
    zero_c = jnp.zeros((bp, mh, HEAD_DIM, HEAD_DIM), F32)
    zero_n = jnp.zeros((bp, mh, HEAD_DIM), F32)
    zero_m = jnp.zeros((bp, mh), F32)
    tm_s = ts

    outs = {k: [] for k in ("kp", "vp", "lfp", "ks", "vs", "lfs", "convp", "convs",
                            "cp", "np", "mp", "cs", "ns", "ms")}
    for l in range(depth):
        (sh1p, sc1p, gt1p, sh2p, sc2p, gt2p), (sh1s, sc1s, gt1s, sh2s, sc2s, gt2s) = _mod_parts(mod_all[l], bp, ls)
        w_main = jnp.concatenate([w_in[l][:, :n_main], w_in[l][:, a0:f0]], axis=1).astype(BF16)
        gate_w = jnp.concatenate([w_in[l][:, g0:a0], w_in[l][:, f0:]], axis=1)
        gate_w = jnp.pad(gate_w, ((0, 0), (0, GATE_COLS - gate_w.shape[1]))).astype(BF16)
        gate_bias = jnp.concatenate([mlstm_i_bias[l], mlstm_f_bias[l], fox_f_bias[l]])
        bias_row = jnp.pad(gate_bias, (0, GATE_COLS - gate_bias.shape[0]))[None, :]
        bias_col = gate_bias[:2 * mh, None]
        w_out_b = w_out[l].astype(BF16)
        w_router_t = w_router[l].T
        wsg, wsu, wsd = w_sh_gate[l].astype(BF16), w_sh_up[l].astype(BF16), w_sh_down[l].astype(BF16)
        gpm, gqm = g_pre_mix[l][None, :], g_post_mix[l][None, :]
        gpf, gqf = g_pre_ffn[l][None, :], g_post_ffn[l][None, :]
        dw, db = conv_dw[l], conv_db[l][None, :]
        lng, lnb = conv_ln_g[l][None, :], conv_ln_b[l][None, :]

        groups = []
        for (x, sc1, sh1, tm) in ((xp, sc1p, sh1p, 512), (xs, sc1s, sh1s, tm_s)):
            u = _inproj(x, gpm, sc1, sh1, w_main, tm, 1024)
            gates = _inproj(x, gpm, sc1, sh1, gate_w, tm, GATE_COLS)
            groups.append((u, gates))
        (u_p, gates_p), (u_s, gates_s) = groups

        yc_p, conv_p = _conv(u_p, zero_conv, dw, db, lng, lnb, bp, lp, min(512, lp))
        yc_s, conv_s = _conv(u_s, state_conv[l], dw, db, lng, lnb, bs, ls, ls)

        def gates_t(gates, nb, seq):
            c = MLSTM_CHUNK if seq % MLSTM_CHUNK == 0 else seq
            return gates[:, :2 * mh].reshape(nb, seq // c, c, 2 * mh).transpose(0, 1, 3, 2)

        mcol0 = (2 * cc) // mw
        ym_p, c_p, n_p, m_p = _mlstm(u_p, gates_p, gates_t(gates_p, bp, lp), bias_row, bias_col,
                                     zero_c, zero_n, zero_m, bp, lp, mcol0)
        ym_s, c_s, n_s, m_s = _mlstm(u_s, gates_s, gates_t(gates_s, bs, ls), bias_row, bias_col,
                                     state_mlstm_C[l], state_mlstm_n[l], state_mlstm_m[l], bs, ls, mcol0)

        fb = fox_f_bias[l]
        lf_p = _log_sigmoid(gates_p[:, 2 * mh:2 * mh + ah] + fb).reshape(bp, lp, ah)
        lf_s = _log_sigmoid(gates_s[:, 2 * mh:2 * mh + ah] + fb).reshape(bs, ls, ah)
        fcum = jnp.cumsum(lf_p, axis=1).transpose(0, 2, 1)
        qcol = n_main // HEAD_DIM
        ya_p = _fox_prompt(u_p, fcum[..., None], fcum[:, :, None, :], bp, lp, ah,
                           qcol, qcol + ah, qcol + 2 * ah)
        k_s = u_s[:, n_main + aw:n_main + 2 * aw]
        v_s = u_s[:, n_main + 2 * aw:n_main + 3 * aw]
        past = _past_bias(cache_lft, page_table, l)
        past = past.transpose(0, 2, 1).reshape(bs, n_pages, 1, page * ah)
        ya_s = _fox_sample(u_s[:, n_main:n_main + aw].reshape(ts * ah, HEAD_DIM),
                           k_s.reshape(ts * ah, HEAD_DIM), v_s.reshape(ts * ah, HEAD_DIM),
                           lf_s.reshape(ts * ah, 1), lf_s.reshape(bs, 1, ls * ah),
                           cache_k2, cache_v2, past, page_table, l, ah).reshape(ts, aw)

        x1_p, h2_p, sco_p = _outproj(yc_p, ym_p, ya_p, xp, w_out_b, gqm, gt1p, gpf, sc2p, sh2p, w_router_t, 256)
        x1_s, h2_s, sco_s = _outproj(yc_s, ym_s, ya_s, xs, w_out_b, gqm, gt1s, gpf, sc2s, sh2s, w_router_t, tm_s)

        idx_t, gw_t, rank_t, counts = _route(jnp.concatenate([sco_p, sco_s], axis=1), router_bias[l])
        row_tok, pos, block_e, n_active = _dispatch(idx_t, rank_t, counts)
        h2_all = jnp.concatenate([h2_p, h2_s], axis=0)
        ysort = _experts(h2_all, row_tok, block_e, n_active, w_exp_gate, w_exp_up, w_exp_down, l)
        gw = gw_t.T
        xp = _combine(ysort, pos[:tp], x1_p, h2_p, gw[:tp], wsg, wsu, wsd, gqf, gt2p, 128)
        xs = _combine(ysort, pos[tp:], x1_s, h2_s, gw[tp:], wsg, wsu, wsd, gqf, gt2s, min(128, ts))

        outs["kp"].append(u_p[:, n_main + aw:n_main + 2 * aw].reshape(bp, lp, ah, HEAD_DIM))
        outs["vp"].append(u_p[:, n_main + 2 * aw:n_main + 3 * aw].reshape(bp, lp, ah, HEAD_DIM))
        outs["lfp"].append(lf_p)
        outs["ks"].append(k_s.reshape(bs, ls, ah, HEAD_DIM))
        outs["vs"].append(v_s.reshape(bs, ls, ah, HEAD_DIM))
        outs["lfs"].append(lf_s)
        outs["convp"].append(conv_p)
        outs["convs"].append(conv_s)
        outs["cp"].append(c_p)
        outs["np"].append(n_p)
        outs["mp"].append(m_p.reshape(bp, mh))
        outs["cs"].append(c_s)
        outs["ns"].append(n_s)
        outs["ms"].append(m_s.reshape(bs, mh))

    st = {k: jnp.stack(v) for k, v in outs.items()}
    return (xp.reshape(bp, lp, d), xs.reshape(bs, ls, d),
            st["kp"], st["vp"], st["lfp"], st["ks"], st["vs"], st["lfs"],
            st["convp"], st["convs"], st["cp"], st["np"], st["mp"], st["cs"], st["ns"], st["ms"])
```

```python
import functools

import jax
import jax.numpy as jnp
from jax import lax
from jax.experimental import pallas as pl
from jax.experimental.pallas import tpu as pltpu

F32 = jnp.float32
BF16 = jnp.bfloat16
I32 = jnp.int32
HI = lax.Precision.HIGHEST

HEAD_DIM = 128
CONV_W = 31
MLSTM_CHUNK = 64
N_EXPERT_GROUPS = 8
TOPK_GROUPS = 4
TOP_K = 8
ROUTED_SCALE = 2.5
NORM_EPS = 1e-6

LANES = 128
GATE_COLS = 128
EXPERT_BLOCK = 256
VMEM_LIMIT = 56 * 1024 * 1024


def _params(sem):
    return pltpu.CompilerParams(dimension_semantics=sem, vmem_limit_bytes=VMEM_LIMIT)


def _log_sigmoid(x):
    return jnp.minimum(x, 0.0) - jnp.log1p(jnp.exp(-jnp.abs(x)))


def _silu(x):
    return x * jax.nn.sigmoid(x)


def _rms(x):
    return x * lax.rsqrt(jnp.mean(x * x, axis=-1, keepdims=True) + NORM_EPS)


def _dot(a, b):
    return jnp.dot(a, b, preferred_element_type=F32)


def _dot_nt(a, b):
    return lax.dot_general(a, b, (((1,), (1,)), ((), ())), preferred_element_type=F32)


def _dot_tn(a, b):
    return lax.dot_general(a, b, (((0,), (0,)), ((), ())), preferred_element_type=F32)


def _dot_hi(a, b):
    return jnp.dot(a, b, preferred_element_type=F32, precision=HI)


def _split3(a):
    hi = a.astype(BF16)
    rest = a - hi.astype(F32)
    mid = rest.astype(BF16)
    lo = (rest - mid.astype(F32)).astype(BF16)
    return hi, mid, lo


def _dot_ones(a, ones):
    ob = ones.astype(BF16)
    hi, mid, lo = _split3(a)
    return _dot(hi, ob) + _dot(mid, ob) + _dot(lo, ob)


def _ones_dot(ones, a):
    ob = ones.astype(BF16)
    hi, mid, lo = _split3(a)
    return _dot(ob, hi) + _dot(ob, mid) + _dot(ob, lo)


def _lane_tile(t, cap):
    best = LANES
    for k in range(1, cap // LANES + 1):
        if t % (k * LANES) == 0:
            best = k * LANES
    return best


def _store_slabs(ref, x):
    rows, d = x.shape
    nslab = d // LANES
    for j in range(nslab):
        ref[pl.ds(j, rows, stride=nslab), :] = x[:, j * LANES:(j + 1) * LANES]


def _load_slabs(ref, rows, nslab, lead=()):
    return jnp.concatenate([ref[lead + (pl.ds(j, rows, stride=nslab), slice(None))] for j in range(nslab)],
                           axis=1)


def _ada_kernel(c_ref, w_ref, b_ref, o_ref):
    s = _silu(c_ref[...]).astype(BF16)
    o_ref[0] = _dot(s, w_ref[0].astype(BF16)) + b_ref[0]


def _ada(c_all, w_ada, b_ada):
    depth, d, n = w_ada.shape
    r = c_all.shape[0]
    tn = 1024
    return pl.pallas_call(
        _ada_kernel,
        grid=(depth, n // tn),
        in_specs=[pl.BlockSpec((r, d), lambda l, j: (0, 0)),
                  pl.BlockSpec((1, d, tn), lambda l, j: (l, 0, j)),
                  pl.BlockSpec((1, 1, tn), lambda l, j: (l, 0, j))],
        out_specs=pl.BlockSpec((1, r, tn), lambda l, j: (l, 0, j)),
        out_shape=jax.ShapeDtypeStruct((depth, r, n), F32),
        compiler_params=_params(("arbitrary", "arbitrary")),
        name="ada",
    )(c_all, w_ada, b_ada.reshape(depth, 1, n))


def _mod_spec(mod, t, tm, row_axis):
    ngrp, r, d = mod.shape
    if r == 1:
        tiles_per_grp = (t // ngrp) // tm
        return pl.BlockSpec((1, 1, d), lambda *ix: (ix[row_axis] // tiles_per_grp, 0, 0))
    return pl.BlockSpec((1, tm, d), lambda *ix: (0, ix[row_axis], 0))


def _inproj_kernel(x_ref, g_ref, sc_ref, sh_ref, w_ref, o_ref):
    h = (_rms(x_ref[...]) * g_ref[...]) * (1.0 + sc_ref[0]) + sh_ref[0]
    o_ref[...] = _dot(h.astype(BF16), w_ref[...])


def _inproj(x, g, sc, sh, w, tm, tn):
    t, d = x.shape
    n = w.shape[1]
    mod_spec = _mod_spec(sc, t, tm, 1)
    return pl.pallas_call(
        _inproj_kernel,
        grid=(n // tn, t // tm),
        in_specs=[pl.BlockSpec((tm, d), lambda j, i: (i, 0)),
                  pl.BlockSpec((1, d), lambda j, i: (0, 0)),
                  mod_spec, mod_spec,
                  pl.BlockSpec((d, tn), lambda j, i: (0, j))],
        out_specs=pl.BlockSpec((tm, tn), lambda j, i: (i, j)),
        out_shape=jax.ShapeDtypeStruct((t, n), F32),
        compiler_params=_params(("arbitrary", "arbitrary")),
        name="inproj",
    )(x, g, sc, sh, w)


def _conv_kernel(a_ref, g_ref, buf_ref, dw_ref, db_ref, lg_ref, lb_ref, y_ref, st_ref, full_ref,
                 *, lc, nchunks):
    halo = CONV_W + 1
    j = pl.program_id(1)

    @pl.when(j == 0)
    def _():
        full_ref[0:2, :] = jnp.zeros((2, full_ref.shape[1]), F32)
        full_ref[2:halo, :] = buf_ref[0]

    u = a_ref[...] * jax.nn.sigmoid(g_ref[...])
    full_ref[halo:halo + lc, :] = u
    acc = jnp.zeros_like(u)
    for w in range(CONV_W):
        acc = acc + full_ref[2 + w:2 + w + lc, :] * dw_ref[w:w + 1, :]
    acc = acc + db_ref[...]
    mu = jnp.mean(acc, axis=-1, keepdims=True)
    cen = acc - mu
    var = jnp.mean(cen * cen, axis=-1, keepdims=True)
    y = cen * lax.rsqrt(var + NORM_EPS) * lg_ref[...] + lb_ref[...]
    y_ref[...] = _silu(y)

    @pl.when(j == nchunks - 1)
    def _():
        st_ref[0] = full_ref[2 + lc:halo + lc, :]

    if nchunks > 1:
        full_ref[0:halo, :] = full_ref[lc:lc + halo, :]


def _conv(u, buf, dw, db, lg, lb, nb, seq, lc):
    c = buf.shape[2]
    nchunks = seq // lc
    vec = pl.BlockSpec((1, c), lambda b, j: (0, 0))
    return pl.pallas_call(
        functools.partial(_conv_kernel, lc=lc, nchunks=nchunks),
        grid=(nb, nchunks),
        in_specs=[pl.BlockSpec((lc, c), lambda b, j: (b * nchunks + j, 0)),
                  pl.BlockSpec((lc, c), lambda b, j: (b * nchunks + j, 1)),
                  pl.BlockSpec((1, CONV_W - 1, c), lambda b, j: (b, 0, 0)),
                  pl.BlockSpec((CONV_W, c), lambda b, j: (0, 0)),
                  vec, vec, vec],
        out_specs=[pl.BlockSpec((lc, c), lambda b, j: (b * nchunks + j, 0)),
                   pl.BlockSpec((1, CONV_W - 1, c), lambda b, j: (b, 0, 0))],
        out_shape=[jax.ShapeDtypeStruct((nb * seq, c), F32),
                   jax.ShapeDtypeStruct((nb, CONV_W - 1, c), F32)],
        scratch_shapes=[pltpu.VMEM((CONV_W + 1 + lc, c), F32)],
        compiler_params=_params(("arbitrary", "arbitrary")),
        name="conv",
    )(u, u, buf, dw, db, lg, lb)


def _mlstm_kernel(q_ref, k_ref, v_ref, o_ref, gc_ref, gr_ref, bc_ref, br_ref, c0_ref, n0_ref, m0_ref,
                  y_ref, cout_ref, nout_ref, mout_ref, c_s, n_s, m_s, *, c, nc, nh, nbs):
    j = pl.program_id(1)

    @pl.when(j == 0)
    def _():
        c_s[...] = c0_ref[...]
        n_s[...] = n0_ref[...]
        for b in range(nbs):
            for h in range(nh):
                m_s[b, h] = jnp.broadcast_to(m0_ref[b][:, h:h + 1], m_s.shape[2:])

    for b in range(nbs):
        _mlstm_chunk(b, q_ref, k_ref, v_ref, o_ref, gc_ref, gr_ref, bc_ref, br_ref, y_ref, c_s, n_s, m_s,
                     c=c, nh=nh)

    @pl.when(j == nc - 1)
    def _():
        cout_ref[...] = c_s[...]
        nout_ref[...] = n_s[...]
        lane = lax.broadcasted_iota(I32, (1, nh), 1)
        for b in range(nbs):
            m_out = jnp.zeros((1, nh), F32)
            for h in range(nh):
                m_out = jnp.where(lane == h, m_s[b, h][0:1, 0:1], m_out)
            mout_ref[b] = m_out


def _mlstm_chunk(b, q_ref, k_ref, v_ref, o_ref, gc_ref, gr_ref, bc_ref, br_ref, y_ref, c_s, n_s, m_s, *, c, nh):
    scale = HEAD_DIM ** -0.5
    mm = BF16 if c % 16 == 0 else F32
    gcol = gc_ref[b] + bc_ref[...]
    grow = gr_ref[b, 0] + br_ref[...]
    lf_col = _log_sigmoid(gcol)
    lf_row = _log_sigmoid(grow)
    row = lax.broadcasted_iota(I32, (c, c), 0)
    col = lax.broadcasted_iota(I32, (c, c), 1)
    causal = col <= row
    if mm == BF16:
        b_col_all = _ones_dot(causal, lf_col)
        b_row_all = _dot_ones(lf_row, row <= col)
    else:
        b_col_all = _dot_hi(causal.astype(F32), lf_col)
        b_row_all = _dot_hi(lf_row, (row <= col).astype(F32))

    for h in range(nh):
        sl = slice(h * HEAD_DIM, (h + 1) * HEAD_DIM)
        q = q_ref[b, :, sl]
        k = k_ref[b, :, sl] * scale
        v = v_ref[b, :, sl]
        b_c = b_col_all[:, nh + h:nh + h + 1]
        b_r = b_row_all[nh + h:nh + h + 1, :]
        ig_r = grow[h:h + 1, :]
        ig_c = gcol[:, h:h + 1]
        m_prev = m_s[b, h][0:1, 0:1]
        dmat = jnp.where(causal, b_c - b_r + ig_r, -jnp.inf)
        inter = b_c + m_prev
        m_t = jnp.maximum(inter, jnp.max(dmat, axis=1, keepdims=True))
        dw = jnp.exp(dmat - m_t)
        iw = jnp.exp(inter - m_t)
        qm, km, vm = q.astype(mm), k.astype(mm), v.astype(mm)
        s = _dot_nt(qm, km) * dw
        c_h = c_s[b, h]
        n_h = n_s[b, h:h + 1, :]
        num = _dot(s.astype(mm), vm) + iw * _dot(qm, c_h.astype(mm))
        den = jnp.sum(s, axis=1, keepdims=True) + iw * jnp.sum(q * n_h, axis=1, keepdims=True)
        hid = num / jnp.maximum(jnp.abs(den), jnp.exp(-m_t))
        y_ref[b, :, sl] = jax.nn.sigmoid(o_ref[b, :, sl]) * hid
        m_last = m_t[c - 1:c, :]
        w_end = jnp.exp(b_c[c - 1:c, :] - b_c + ig_c - m_last)
        f_end = iw[c - 1:c, :]
        kw = k * w_end
        c_s[b, h] = f_end * c_h + _dot_tn(kw.astype(mm), vm)
        n_s[b, h:h + 1, :] = f_end * n_h + jnp.sum(kw, axis=0, keepdims=True)
        m_s[b, h] = jnp.broadcast_to(m_last, m_s.shape[2:])


def _mlstm(u, gates, gates_t, bias_row, bias_col, c0, n0, m0, nb, seq, col0):
    nh = c0.shape[1]
    w = nh * HEAD_DIM
    c = MLSTM_CHUNK if seq % MLSTM_CHUNK == 0 else seq
    nc = seq // c
    nbs = 4 if nb % 4 == 0 else 1
    u3 = u.reshape(nb, seq, u.shape[1])

    def ublk(off):
        return pl.BlockSpec((nbs, c, w), lambda g, j: (g, j, col0 + off))

    y, c_out, n_out, m_out = pl.pallas_call(
        functools.partial(_mlstm_kernel, c=c, nc=nc, nh=nh, nbs=nbs),
        grid=(nb // nbs, nc),
        in_specs=[ublk(0), ublk(1), ublk(2), ublk(3),
                  pl.BlockSpec((nbs, c, GATE_COLS), lambda g, j: (g, j, 0)),
                  pl.BlockSpec((nbs, 1, 2 * nh, c), lambda g, j: (g, j, 0, 0)),
                  pl.BlockSpec((1, GATE_COLS), lambda g, j: (0, 0)),
                  pl.BlockSpec((2 * nh, 1), lambda g, j: (0, 0)),
                  pl.BlockSpec((nbs, nh, HEAD_DIM, HEAD_DIM), lambda g, j: (g, 0, 0, 0)),
                  pl.BlockSpec((nbs, nh, HEAD_DIM), lambda g, j: (g, 0, 0)),
                  pl.BlockSpec((nbs, 1, nh), lambda g, j: (g, 0, 0))],
        out_specs=[pl.BlockSpec((nbs, c, w), lambda g, j: (g, j, 0)),
                   pl.BlockSpec((nbs, nh, HEAD_DIM, HEAD_DIM), lambda g, j: (g, 0, 0, 0)),
                   pl.BlockSpec((nbs, nh, HEAD_DIM), lambda g, j: (g, 0, 0)),
                   pl.BlockSpec((nbs, 1, nh), lambda g, j: (g, 0, 0))],
        out_shape=[jax.ShapeDtypeStruct((nb, seq, w), F32),
                   jax.ShapeDtypeStruct((nb, nh, HEAD_DIM, HEAD_DIM), F32),
                   jax.ShapeDtypeStruct((nb, nh, HEAD_DIM), F32),
                   jax.ShapeDtypeStruct((nb, 1, nh), F32)],
        scratch_shapes=[pltpu.VMEM((nbs, nh, HEAD_DIM, HEAD_DIM), F32),
                        pltpu.VMEM((nbs, nh, HEAD_DIM), F32),
                        pltpu.VMEM((nbs, nh, 8, LANES), F32)],
        compiler_params=_params(("arbitrary", "arbitrary")),
        name="mlstm",
    )(u3, u3, u3, u3, gates.reshape(nb, seq, GATE_COLS), gates_t, bias_row, bias_col, c0, n0,
      m0.reshape(nb, 1, nh))
    return y.reshape(nb * seq, w), c_out, n_out, m_out


def _fox_prompt_kernel(q_ref, k_ref, v_ref, fc_ref, fr_ref, o_ref, *, seq, tq):
    scale = HEAD_DIM ** -0.5
    kb = k_ref[...].astype(BF16)
    vb = v_ref[...].astype(BF16)
    f_row = fr_ref[0, 0]
    for qi in range(seq // tq):
        rows = slice(qi * tq, (qi + 1) * tq)
        kend = (qi + 1) * tq
        q = (q_ref[rows, :] * scale).astype(BF16)
        s = _dot_nt(q, kb[:kend]) + fc_ref[0, 0][rows, :] - f_row[:, :kend]
        qpos = lax.broadcasted_iota(I32, (tq, kend), 0) + qi * tq
        kpos = lax.broadcasted_iota(I32, (tq, kend), 1)
        s = jnp.where(kpos <= qpos, s, -jnp.inf)
        p = jnp.exp(s - jnp.max(s, axis=1, keepdims=True))
        denom = jnp.sum(p, axis=1, keepdims=True)
        o_ref[rows, :] = _dot(p.astype(BF16), vb[:kend]) / denom


def _fox_prompt(u, f_col, f_row, nb, seq, nh, qcol, kcol, vcol):
    tq = min(256, seq)
    return pl.pallas_call(
        functools.partial(_fox_prompt_kernel, seq=seq, tq=tq),
        grid=(nb, nh),
        in_specs=[pl.BlockSpec((seq, HEAD_DIM), lambda b, h: (b, qcol + h)),
                  pl.BlockSpec((seq, HEAD_DIM), lambda b, h: (b, kcol + h)),
                  pl.BlockSpec((seq, HEAD_DIM), lambda b, h: (b, vcol + h)),
                  pl.BlockSpec((1, 1, seq, 1), lambda b, h: (b, h, 0, 0)),
                  pl.BlockSpec((1, 1, 1, seq), lambda b, h: (b, h, 0, 0))],
        out_specs=pl.BlockSpec((seq, HEAD_DIM), lambda b, h: (b, h)),
        out_shape=jax.ShapeDtypeStruct((nb * seq, nh * HEAD_DIM), F32),
        compiler_params=_params(("arbitrary", "arbitrary")),
        name="fox_prompt",
    )(u, u, u, f_col, f_row)


def _pages_per_step(n_pages):
    return 8 if n_pages % 8 == 0 else 1


def _past_bias_kernel(pt_ref, *rest, npg):
    lf_refs = rest[:npg]
    o_ref, carry_s = rest[npg:]
    page = lf_refs[0].shape[3]

    @pl.when(pl.program_id(1) == 0)
    def _():
        carry_s[...] = jnp.zeros(carry_s.shape, F32)

    later = lax.broadcasted_iota(I32, (page, page), 0) > lax.broadcasted_iota(I32, (page, page), 1)
    carry = carry_s[...]
    for r in reversed(range(npg)):
        lft = lf_refs[r][0, 0]
        o_ref[0, :, r * page:(r + 1) * page] = carry + _dot_ones(lft, later)
        carry = carry + jnp.sum(lft, axis=1, keepdims=True)
    carry_s[...] = carry


def _past_bias(cache_lft, page_table, layer):
    nb, n_pages = page_table.shape
    nh, page = cache_lft.shape[2], cache_lft.shape[3]
    npg = 32 if n_pages % 32 == 0 else _pages_per_step(n_pages)
    ngrp = n_pages // npg
    specs = [pl.BlockSpec((1, 1, nh, page),
                          lambda b, g, pt, r=r: (layer, pt[b, (ngrp - 1 - g) * npg + r], 0, 0))
             for r in range(npg)]
    grid_spec = pltpu.PrefetchScalarGridSpec(
        num_scalar_prefetch=1,
        grid=(nb, ngrp),
        in_specs=specs,
        out_specs=pl.BlockSpec((1, nh, npg * page), lambda b, g, pt: (b, 0, ngrp - 1 - g)),
        scratch_shapes=[pltpu.VMEM((nh, 1), F32)],
    )
    return pl.pallas_call(
        functools.partial(_past_bias_kernel, npg=npg),
        grid_spec=grid_spec,
        out_shape=jax.ShapeDtypeStruct((nb, nh, n_pages * page), F32),
        compiler_params=_params(("arbitrary", "arbitrary")),
        name="fox_past_bias",
    )(page_table, *([cache_lft] * npg))


def _fox_sample_kernel(pt_ref, q_ref, kn_ref, vn_ref, lfc_ref, lfr_ref, *rest, npg, ngrp, nh):
    k_refs = rest[0:npg]
    v_refs = rest[npg:2 * npg]
    b_refs = rest[2 * npg:3 * npg]
    o_ref, m_s, l_s, acc_s = rest[3 * npg:]
    g = pl.program_id(1)
    scale = HEAD_DIM ** -0.5
    nr = q_ref.shape[0]
    ncol = k_refs[0].shape[2]
    hshift = nh.bit_length() - 1
    qb = (q_ref[...] * scale).astype(BF16)

    ri = lax.broadcasted_iota(I32, (nr, nr), 0)
    ci = lax.broadcasted_iota(I32, (nr, nr), 1)
    same_head = (ri & (nh - 1)) == (ci & (nh - 1))
    col_le_row = jnp.logical_and(same_head, (ci >> hshift) <= (ri >> hshift))
    row_le_col = jnp.logical_and(same_head, (ri >> hshift) <= (ci >> hshift))
    f_col = jnp.sum(jnp.where(col_le_row, lfr_ref[0], 0.0), axis=1, keepdims=True)

    @pl.when(g == 0)
    def _():
        f_row = jnp.sum(jnp.where(row_le_col, lfc_ref[...], 0.0), axis=0, keepdims=True)
        s = _dot_nt(qb, kn_ref[...].astype(BF16)) + f_col - f_row
        s = jnp.where(col_le_row, s, -jnp.inf)
        m = jnp.max(s, axis=1, keepdims=True)
        p = jnp.exp(s - m)
        m_s[...] = m
        l_s[...] = jnp.sum(p, axis=1, keepdims=True)
        acc_s[...] = _dot(p.astype(BF16), vn_ref[...].astype(BF16))

    rr = lax.broadcasted_iota(I32, (nr, ncol), 0)
    cc = lax.broadcasted_iota(I32, (nr, ncol), 1)
    base = f_col + jnp.where((rr & (nh - 1)) == (cc & (nh - 1)), 0.0, -jnp.inf)
    s = jnp.concatenate([_dot_nt(qb, k_refs[r][0, 0].astype(BF16)) + (base + b_refs[r][0, 0])
                         for r in range(npg)], axis=1)
    m_old = m_s[...]
    m_new = jnp.maximum(m_old, jnp.max(s, axis=1, keepdims=True))
    alpha = jnp.exp(m_old - m_new)
    p = jnp.exp(s - m_new)
    l_s[...] = alpha * l_s[...] + jnp.sum(p, axis=1, keepdims=True)
    pv = _dot(p[:, 0:ncol].astype(BF16), v_refs[0][0, 0].astype(BF16))
    for r in range(1, npg):
        pv = pv + _dot(p[:, r * ncol:(r + 1) * ncol].astype(BF16), v_refs[r][0, 0].astype(BF16))
    acc_s[...] = alpha * acc_s[...] + pv
    m_s[...] = m_new

    @pl.when(g == ngrp - 1)
    def _():
        o_ref[...] = acc_s[...] / l_s[...]


def _fox_sample(q2, kn2, vn2, lf_col, lf_row, cache_k2, cache_v2, bias, page_table, layer, nh):
    nb, n_pages = page_table.shape
    nr = q2.shape[0] // nb
    ncol = cache_k2.shape[2]
    npg = _pages_per_step(n_pages)
    ngrp = n_pages // npg
    assert nh & (nh - 1) == 0

    kv_specs = [pl.BlockSpec((1, 1, ncol, HEAD_DIM), lambda b, g, pt, r=r: (layer, pt[b, g * npg + r], 0, 0))
                for r in range(npg)]
    b_specs = [pl.BlockSpec((1, 1, 1, ncol), lambda b, g, pt, r=r: (b, g * npg + r, 0, 0)) for r in range(npg)]
    rows = pl.BlockSpec((nr, HEAD_DIM), lambda b, g, pt: (b, 0))
    grid_spec = pltpu.PrefetchScalarGridSpec(
        num_scalar_prefetch=1,
        grid=(nb, ngrp),
        in_specs=[rows, rows, rows,
                  pl.BlockSpec((nr, 1), lambda b, g, pt: (b, 0)),
                  pl.BlockSpec((1, 1, nr), lambda b, g, pt: (b, 0, 0))] + kv_specs + kv_specs + b_specs,
        out_specs=rows,
        scratch_shapes=[pltpu.VMEM((nr, 1), F32), pltpu.VMEM((nr, 1), F32), pltpu.VMEM((nr, HEAD_DIM), F32)],
    )
    return pl.pallas_call(
        functools.partial(_fox_sample_kernel, npg=npg, ngrp=ngrp, nh=nh),
        grid_spec=grid_spec,
        out_shape=jax.ShapeDtypeStruct(q2.shape, F32),
        compiler_params=_params(("arbitrary", "arbitrary")),
        name="fox_sample",
    )(page_table, q2, kn2, vn2, lf_col, lf_row, *([cache_k2] * npg), *([cache_v2] * npg), *([bias] * npg))


def _outproj_kernel(yc_ref, ym_ref, ya_ref, x_ref, wc_ref, wm_ref, wa_ref, gpost_ref, gt_ref, gpre_ref,
                    sc_ref, sh_ref, wrt_ref, x1_ref, h2_ref, h2s_ref, sco_ref):
    mix = (_dot(yc_ref[...].astype(BF16), wc_ref[...]) + _dot(ym_ref[...].astype(BF16), wm_ref[...])
           + _dot(ya_ref[...].astype(BF16), wa_ref[...]))
    x1 = x_ref[...] + gt_ref[0] * (_rms(mix) * gpost_ref[...])
    x1_ref[...] = x1
    h2 = (_rms(x1) * gpre_ref[...]) * (1.0 + sc_ref[0]) + sh_ref[0]
    h2_ref[...] = h2
    _store_slabs(h2s_ref, h2)
    logits_t = lax.dot_general(wrt_ref[...], h2, (((1,), (1,)), ((), ())),
                               preferred_element_type=F32, precision=HI)
    sco_ref[...] = jax.nn.sigmoid(logits_t)


def _outproj(yc, ym, ya, x, w_out, gpost, gt, gpre, sc, sh, w_router_t, tm):
    t, d = x.shape
    cw, mw, aw = yc.shape[1], ym.shape[1], ya.shape[1]
    ne = w_router_t.shape[0]
    mod_spec = _mod_spec(sc, t, tm, 0)
    vec = pl.BlockSpec((1, d), lambda i: (0, 0))
    return pl.pallas_call(
        _outproj_kernel,
        grid=(t // tm,),
        in_specs=[pl.BlockSpec((tm, cw), lambda i: (i, 0)),
                  pl.BlockSpec((tm, mw), lambda i: (i, 0)),
                  pl.BlockSpec((tm, aw), lambda i: (i, 0)),
                  pl.BlockSpec((tm, d), lambda i: (i, 0)),
                  pl.BlockSpec((cw, d), lambda i: (0, 0)),
                  pl.BlockSpec((mw, d), lambda i: (cw // mw, 0)),
                  pl.BlockSpec((aw, d), lambda i: ((cw + mw) // aw, 0)),
                  vec, mod_spec, vec, mod_spec, mod_spec,
                  pl.BlockSpec((ne, d), lambda i: (0, 0))],
        out_specs=[pl.BlockSpec((tm, d), lambda i: (i, 0)),
                   pl.BlockSpec((tm, d), lambda i: (i, 0)),
                   pl.BlockSpec((tm * (d // LANES), LANES), lambda i: (i, 0)),
                   pl.BlockSpec((ne, tm), lambda i: (0, i))],
        out_shape=[jax.ShapeDtypeStruct((t, d), F32),
                   jax.ShapeDtypeStruct((t, d), F32),
                   jax.ShapeDtypeStruct((t * (d // LANES), LANES), F32),
                   jax.ShapeDtypeStruct((ne, t), F32)],
        compiler_params=_params(("arbitrary",)),
        name="outproj",
    )(yc, ym, ya, x, w_out, w_out, w_out, gpost, gt, gpre, sc, sh, w_router_t)


def _route_kernel(s_ref, b_ref, idx_ref, w_ref, rank_ref, cnt_ref, carry_s, *, ne, ng, tt, ntiles):
    i = pl.program_id(0)
    gs = ne // ng

    @pl.when(i == 0)
    def _():
        carry_s[...] = jnp.zeros(carry_s.shape, F32)

    def first_max(x, ids, n):
        m = jnp.max(x, axis=0, keepdims=True)
        return m, jnp.min(jnp.where(x == m, ids, float(n)), axis=0, keepdims=True)

    s = s_ref[...]
    sel = s + b_ref[...]
    sub = lax.broadcasted_iota(I32, (gs, tt), 0).astype(F32)
    grow = lax.broadcasted_iota(I32, (ng, tt), 0).astype(F32)
    gscore = jnp.zeros((ng, tt), F32)
    for g in range(ng):
        blk = sel[g * gs:(g + 1) * gs, :]
        m1, i1 = first_max(blk, sub, gs)
        m2 = jnp.max(jnp.where(sub == i1, -jnp.inf, blk), axis=0, keepdims=True)
        gscore = jnp.where(grow == g, m1 + m2, gscore)
    gsel = jnp.zeros((ng, tt), F32)
    for _ in range(TOPK_GROUPS):
        _, ig = first_max(gscore, grow, ng)
        hit = grow == ig
        gsel = jnp.where(hit, 1.0, gsel)
        gscore = jnp.where(hit, -jnp.inf, gscore)
    cand = jnp.concatenate([jnp.where(gsel[g:g + 1, :] > 0.0, sel[g * gs:(g + 1) * gs, :], -jnp.inf)
                            for g in range(ng)], axis=0)

    erow = lax.broadcasted_iota(I32, (ne, tt), 0).astype(F32)
    member = jnp.zeros((ne, tt), F32)
    picks, weights = [], []
    for _ in range(TOP_K):
        _, ie = first_max(cand, erow, ne)
        hit = erow == ie
        picks.append(ie)
        weights.append(jnp.sum(jnp.where(hit, s, 0.0), axis=0, keepdims=True))
        member = jnp.where(hit, 1.0, member)
        cand = jnp.where(hit, -jnp.inf, cand)
    wsum = weights[0]
    for k in range(1, TOP_K):
        wsum = wsum + weights[k]

    upto = lax.broadcasted_iota(I32, (tt, tt), 0) <= lax.broadcasted_iota(I32, (tt, tt), 1)
    incl = _dot(member.astype(BF16), upto.astype(BF16)) + carry_s[...]
    rank = incl - member
    carry_s[...] = incl[:, tt - 1:tt]

    krow = lax.broadcasted_iota(I32, (TOP_K, tt), 0)
    idx_out = jnp.zeros((TOP_K, tt), F32)
    w_out = jnp.zeros((TOP_K, tt), F32)
    rank_out = jnp.zeros((TOP_K, tt), F32)
    for k in range(TOP_K):
        idx_out = jnp.where(krow == k, picks[k], idx_out)
        w_out = jnp.where(krow == k, weights[k] / wsum * ROUTED_SCALE, w_out)
        rk = jnp.sum(jnp.where(erow == picks[k], rank, 0.0), axis=0, keepdims=True)
        rank_out = jnp.where(krow == k, rk, rank_out)
    idx_ref[...] = idx_out.astype(I32)
    w_ref[...] = w_out
    rank_ref[...] = rank_out

    @pl.when(i == ntiles - 1)
    def _():
        cnt_ref[...] = carry_s[...]


def _route(scores_t, router_bias):
    ne, t = scores_t.shape
    tt = _lane_tile(t, 768)
    ntiles = t // tt
    tok = pl.BlockSpec((TOP_K, tt), lambda i: (0, i))
    return pl.pallas_call(
        functools.partial(_route_kernel, ne=ne, ng=N_EXPERT_GROUPS, tt=tt, ntiles=ntiles),
        grid=(ntiles,),
        in_specs=[pl.BlockSpec((ne, tt), lambda i: (0, i)),
                  pl.BlockSpec((ne, 1), lambda i: (0, 0))],
        out_specs=[tok, tok, tok, pl.BlockSpec((ne, 1), lambda i: (0, 0))],
        out_shape=[jax.ShapeDtypeStruct((TOP_K, t), I32),
                   jax.ShapeDtypeStruct((TOP_K, t), F32),
                   jax.ShapeDtypeStruct((TOP_K, t), F32),
                   jax.ShapeDtypeStruct((ne, 1), F32)],
        scratch_shapes=[pltpu.VMEM((ne, 1), F32)],
        compiler_params=_params(("arbitrary",)),
        name="moe_route",
    )(scores_t, router_bias.astype(F32)[:, None])


def _dispatch(idx_t, rank_t, counts):
    topk, t = idx_t.shape
    ne = counts.shape[0]
    a = t * topk
    counts = counts.reshape(ne).astype(I32)
    padded = (counts + EXPERT_BLOCK - 1) // EXPERT_BLOCK * EXPERT_BLOCK
    pends = jnp.cumsum(padded)
    pstarts = pends - padded
    start_of = jnp.sum(jnp.where(idx_t[:, :, None] == jnp.arange(ne, dtype=I32), pstarts, 0), axis=2)
    pos = (start_of + rank_t.astype(I32)).T
    nblk = -(-a // EXPERT_BLOCK) + ne
    tok = jnp.broadcast_to(jnp.arange(t, dtype=I32)[:, None], (t, topk))
    row_tok = jnp.zeros((nblk * EXPERT_BLOCK,), I32).at[pos.reshape(a)].set(tok.reshape(a), unique_indices=True)
    n_active = pends[-1] // EXPERT_BLOCK
    blk = jnp.minimum(jnp.arange(nblk, dtype=I32), n_active - 1)
    block_e = jnp.minimum(jnp.sum((pends[None, :] <= (blk * EXPERT_BLOCK)[:, None]).astype(I32), axis=1), ne - 1)
    return row_tok, pos, block_e, n_active.reshape(1)


def _experts_kernel(be_ref, nact_ref, first_ref, nxt_ref, wsl_ref, idx_ref, idxn_ref, h_ref,
                    wg_hbm, wu_hbm, wd_hbm, y_ref,
                    x_s, wg_f, wu_f, wd_f, wg_s, wu_s, wd_s, xsem, wsem, *, layer):
    i = pl.program_id(0)
    nact = nact_ref[0]
    active = i < nact
    has_next = i + 1 < nact
    slot = lax.rem(i, 2)
    nslab = wg_s.shape[0] // LANES
    nrows = x_s.shape[1] // nslab

    def row_copy(ids_ref, r, sl):
        src = h_ref.at[pl.ds(pl.multiple_of(ids_ref[0, 0, r], nslab), nslab)]
        return pltpu.make_async_copy(src, x_s.at[sl, pl.ds(r * nslab, nslab)], xsem.at[sl])

    def wait_rows(sl):
        pltpu.make_async_copy(h_ref.at[pl.ds(0, nrows * nslab)], x_s.at[sl], xsem.at[sl]).wait()

    def weight_copies(e, sl):
        return (pltpu.make_async_copy(wg_hbm.at[layer, e], wg_f.at[sl], wsem.at[sl, 0]),
                pltpu.make_async_copy(wu_hbm.at[layer, e], wu_f.at[sl], wsem.at[sl, 1]),
                pltpu.make_async_copy(wd_hbm.at[layer, e], wd_f.at[sl], wsem.at[sl, 2]))

    @pl.when(jnp.logical_and(i == 0, active))
    def _():
        def body(r, carry):
            row_copy(idx_ref, r, 0).start()
            return carry
        lax.fori_loop(0, nrows, body, 0)
        for cp in weight_copies(be_ref[0], 0):
            cp.start()

    @pl.when(jnp.logical_and(active, first_ref[i] == 1))
    def _():
        wsl = wsl_ref[i]

        @pl.when(nxt_ref[i] >= 0)
        def _():
            for cp in weight_copies(nxt_ref[i], 1 - wsl):
                cp.start()

        for cp in weight_copies(be_ref[i], wsl):
            cp.wait()
        wg_s[...] = wg_f[wsl].astype(BF16)
        wu_s[...] = wu_f[wsl].astype(BF16)
        wd_s[...] = wd_f[wsl].astype(BF16)

    def compute(sl, prefetch):
        wait_rows(sl)
        x = _load_slabs(x_s, nrows, nslab, lead=(sl,)).astype(BF16)
        if prefetch:
            for r in range(nrows):
                row_copy(idxn_ref, r, 1 - sl).start()
        hid = _silu(_dot(x, wg_s[...])) * _dot(x, wu_s[...])
        _store_slabs(y_ref, _dot(hid.astype(BF16), wd_s[...]))

    for sl in (0, 1):
        @pl.when(jnp.logical_and(has_next, slot == sl))
        def _(sl=sl):
            compute(sl, True)

        @pl.when(jnp.logical_and(jnp.logical_and(active, jnp.logical_not(has_next)), slot == sl))
        def _(sl=sl):
            compute(sl, False)

    @pl.when(jnp.logical_not(active))
    def _():
        y_ref[...] = jnp.zeros(y_ref.shape, F32)


def _experts(h2, row_tok, block_e, n_active, w_gate, w_up, w_down, layer):
    d, de = w_gate.shape[2], w_gate.shape[3]
    nslab = d // LANES
    nblk = row_tok.shape[0] // EXPERT_BLOCK
    ids = (row_tok * nslab).reshape(nblk, 1, EXPERT_BLOCK)
    blk = jnp.arange(nblk, dtype=I32)
    first = jnp.logical_and(blk < n_active[0],
                            jnp.concatenate([jnp.ones((1,), bool), block_e[1:] != block_e[:-1]]))
    wslot = lax.rem(jnp.cumsum(first.astype(I32)) - 1, 2).astype(I32)
    first_at = jnp.where(first, blk, nblk)
    next_first = lax.cummin(jnp.concatenate([first_at[1:], jnp.full((1,), nblk, I32)]), reverse=True)
    nxt_e = jnp.where(next_first < nblk, block_e[jnp.minimum(next_first, nblk - 1)], -1).astype(I32)
    any_spec = pl.BlockSpec(memory_space=pl.ANY)
    grid_spec = pltpu.PrefetchScalarGridSpec(
        num_scalar_prefetch=5,
        grid=(nblk,),
        in_specs=[pl.BlockSpec((1, 1, EXPERT_BLOCK), lambda i, *_: (i, 0, 0), memory_space=pltpu.SMEM),
                  pl.BlockSpec((1, 1, EXPERT_BLOCK), lambda i, *_: (jnp.minimum(i + 1, nblk - 1), 0, 0),
                               memory_space=pltpu.SMEM),
                  any_spec, any_spec, any_spec, any_spec],
        out_specs=pl.BlockSpec((EXPERT_BLOCK * nslab, LANES), lambda i, *_: (i, 0)),
        scratch_shapes=[pltpu.VMEM((2, EXPERT_BLOCK * nslab, LANES), F32),
                        pltpu.VMEM((2, d, de), F32), pltpu.VMEM((2, d, de), F32), pltpu.VMEM((2, de, d), F32),
                        pltpu.VMEM((d, de), BF16), pltpu.VMEM((d, de), BF16), pltpu.VMEM((de, d), BF16),
                        pltpu.SemaphoreType.DMA((2,)), pltpu.SemaphoreType.DMA((2, 3))],
    )
    return pl.pallas_call(
        functools.partial(_experts_kernel, layer=layer),
        grid_spec=grid_spec,
        out_shape=jax.ShapeDtypeStruct((nblk * EXPERT_BLOCK * nslab, LANES), F32),
        compiler_params=_params(("arbitrary",)),
        name="moe_experts",
    )(block_e, n_active, first.astype(I32), nxt_e, wslot, ids, ids, h2, w_gate, w_up, w_down)


def _combine_kernel(pos_ref, ys_ref, x1_ref, h2_ref, gw_ref, wsg_ref, wsu_ref, wsd_ref, gpost_ref, gt_ref,
                    o_ref, rows_s, mix_s, sem, *, tt, topk):
    nslab = o_ref.shape[1] // LANES
    for n in range(tt * topk):
        src = ys_ref.at[pl.ds(pl.multiple_of(pos_ref[0, 0, n], nslab), nslab)]
        pltpu.make_async_copy(src, rows_s.at[pl.ds(n * nslab, nslab)], sem).start()
    h2 = h2_ref[...].astype(BF16)
    shared = _dot((_silu(_dot(h2, wsg_ref[...])) * _dot(h2, wsu_ref[...])).astype(BF16), wsd_ref[...])
    pltpu.make_async_copy(ys_ref.at[pl.ds(0, tt * topk * nslab)], rows_s, sem).wait()
    for t in range(tt):
        acc = None
        for k in range(topk):
            n = t * topk + k
            term = gw_ref[0, 0, n] * rows_s[n * nslab:(n + 1) * nslab, :]
            acc = term if acc is None else acc + term
        mix_s[t * nslab:(t + 1) * nslab, :] = acc
    f = shared + _load_slabs(mix_s, tt, nslab)
    o_ref[...] = x1_ref[...] + gt_ref[0] * (_rms(f) * gpost_ref[...])


def _combine(ys, pos, x1, h2, gw, wsg, wsu, wsd, gpost, gt, tt):
    t, d = x1.shape
    topk = gw.shape[1]
    de = wsg.shape[1]
    nslab = d // LANES
    scalars = pl.BlockSpec((1, 1, tt * topk), lambda i: (i, 0, 0), memory_space=pltpu.SMEM)
    return pl.pallas_call(
        functools.partial(_combine_kernel, tt=tt, topk=topk),
        grid=(t // tt,),
        in_specs=[scalars,
                  pl.BlockSpec(memory_space=pl.ANY),
                  pl.BlockSpec((tt, d), lambda i: (i, 0)),
                  pl.BlockSpec((tt, d), lambda i: (i, 0)),
                  scalars,
                  pl.BlockSpec((d, de), lambda i: (0, 0)),
                  pl.BlockSpec((d, de), lambda i: (0, 0)),
                  pl.BlockSpec((de, d), lambda i: (0, 0)),
                  pl.BlockSpec((1, d), lambda i: (0, 0)),
                  _mod_spec(gt, t, tt, 0)],
        out_specs=pl.BlockSpec((tt, d), lambda i: (i, 0)),
        out_shape=jax.ShapeDtypeStruct((t, d), F32),
        scratch_shapes=[pltpu.VMEM((tt * topk * nslab, LANES), F32), pltpu.VMEM((tt * nslab, LANES), F32),
                        pltpu.SemaphoreType.DMA(())],
        compiler_params=_params(("arbitrary",)),
        name="moe_combine",
    )((pos * nslab).reshape(t // tt, 1, tt * topk), ys, x1, h2, gw.reshape(t // tt, 1, tt * topk),
      wsg, wsu, wsd, gpost, gt)


def _mod_parts(mod, nb_p, reps):
    parts = jnp.split(mod, 6, axis=-1)
    prompt = [m[:nb_p, None, :] for m in parts]
    sample = [jnp.repeat(m[nb_p:], reps, axis=0)[None] for m in parts]
    return prompt, sample


def kernel(x_prompt, x_sample, cache_k, cache_v, cache_logf, state_conv, state_mlstm_C, state_mlstm_n,
           state_mlstm_m, page_table, c_prompt, c_sample, w_ada, b_ada, g_pre_mix, g_post_mix, g_pre_ffn,
           g_post_ffn, w_in, w_out, conv_dw, conv_db, conv_ln_g, conv_ln_b, mlstm_i_bias, mlstm_f_bias,
           fox_f_bias, w_router, router_bias, w_exp_gate, w_exp_up, w_exp_down, w_sh_gate, w_sh_up, w_sh_down):
    bp, lp, d = x_prompt.shape
    bs, ls, _ = x_sample.shape
    depth = w_ada.shape[0]
    cc = conv_dw.shape[2]
    mh = state_mlstm_C.shape[2]
    mw = mh * HEAD_DIM
    ah = cache_k.shape[3]
    aw = ah * HEAD_DIM
    n_phys, page = cache_k.shape[1], cache_k.shape[2]
    n_pages = page_table.shape[1]
    tp, ts = bp * lp, bs * ls
    n_main = 2 * cc + 4 * mw
    g0 = n_main
    a0 = g0 + 2 * mh
    f0 = a0 + 3 * aw

    xp = x_prompt.reshape(tp, d)
    xs = x_sample.reshape(ts, d)
    mod_all = _ada(jnp.concatenate([c_prompt, c_sample], axis=0), w_ada, b_ada)
    cache_k2 = cache_k.reshape(depth, n_phys, page * ah, HEAD_DIM)
    cache_v2 = cache_v.reshape(depth, n_phys, page * ah, HEAD_DIM)
    cache_lft = jnp.swapaxes(cache_logf, 2, 3)
    zero_conv = jnp.zeros((bp, CONV_W - 1, cc), F32)
    zero_c = jnp.zeros((bp, mh, HEAD_DIM, HEAD_DIM), F32)
    zero_n = jnp.zeros((bp, mh, HEAD_DIM), F32)
    zero_m = jnp.zeros((bp, mh), F32)
    tm_s = ts

    outs = {k: [] for k in ("kp", "vp", "lfp", "ks", "vs", "lfs", "convp", "convs",
                            "cp", "np", "mp", "cs", "ns", "ms")}
    for l in range(depth):
        (sh1p, sc1p, gt1p, sh2p, sc2p, gt2p), (sh1s, sc1s, gt1s, sh2s, sc2s, gt2s) = _mod_parts(mod_all[l], bp, ls)
        w_main = jnp.concatenate([w_in[l][:, :n_main], w_in[l][:, a0:f0]], axis=1).astype(BF16)
        gate_w = jnp.concatenate([w_in[l][:, g0:a0], w_in[l][:, f0:]], axis=1)
        gate_w = jnp.pad(gate_w, ((0, 0), (0, GATE_COLS - gate_w.shape[1]))).astype(BF16)
        gate_bias = jnp.concatenate([mlstm_i_bias[l], mlstm_f_bias[l], fox_f_bias[l]])
        bias_row = jnp.pad(gate_bias, (0, GATE_COLS - gate_bias.shape[0]))[None, :]
        bias_col = gate_bias[:2 * mh, None]
        w_out_b = w_out[l].astype(BF16)
        w_router_t = w_router[l].T
        wsg, wsu, wsd = w_sh_gate[l].astype(BF16), w_sh_up[l].astype(BF16), w_sh_down[l].astype(BF16)
        gpm, gqm = g_pre_mix[l][None, :], g_post_mix[l][None, :]
        gpf, gqf = g_pre_ffn[l][None, :], g_post_ffn[l][None, :]
        dw, db = conv_dw[l], conv_db[l][None, :]
        lng, lnb = conv_ln_g[l][None, :], conv_ln_b[l][None, :]

        groups = []
        for (x, sc1, sh1, tm) in ((xp, sc1p, sh1p, 512), (xs, sc1s, sh1s, tm_s)):
            u = _inproj(x, gpm, sc1, sh1, w_main, tm, 1024)
            gates = _inproj(x, gpm, sc1, sh1, gate_w, tm, GATE_COLS)
            groups.append((u, gates))
        (u_p, gates_p), (u_s, gates_s) = groups

        yc_p, conv_p = _conv(u_p, zero_conv, dw, db, lng, lnb, bp, lp, min(512, lp))
        yc_s, conv_s = _conv(u_s, state_conv[l], dw, db, lng, lnb, bs, ls, ls)

        def gates_t(gates, nb, seq):
            c = MLSTM_CHUNK if seq % MLSTM_CHUNK == 0 else seq
            return gates[:, :2 * mh].reshape(nb, seq // c, c, 2 * mh).transpose(0, 1, 3, 2)

        mcol0 = (2 * cc) // mw
        ym_p, c_p, n_p, m_p = _mlstm(u_p, gates_p, gates_t(gates_p, bp, lp), bias_row, bias_col,
                                     zero_c, zero_n, zero_m, bp, lp, mcol0)
        ym_s, c_s, n_s, m_s = _mlstm(u_s, gates_s, gates_t(gates_s, bs, ls), bias_row, bias_col,
                                     state_mlstm_C[l], state_mlstm_n[l], state_mlstm_m[l], bs, ls, mcol0)

        fb = fox_f_bias[l]
        lf_p = _log_sigmoid(gates_p[:, 2 * mh:2 * mh + ah] + fb).reshape(bp, lp, ah)
        lf_s = _log_sigmoid(gates_s[:, 2 * mh:2 * mh + ah] + fb).reshape(bs, ls, ah)
        fcum = jnp.cumsum(lf_p, axis=1).transpose(0, 2, 1)
        qcol = n_main // HEAD_DIM
        ya_p = _fox_prompt(u_p, fcum[..., None], fcum[:, :, None, :], bp, lp, ah,
                           qcol, qcol + ah, qcol + 2 * ah)
        k_s = u_s[:, n_main + aw:n_main + 2 * aw]
        v_s = u_s[:, n_main + 2 * aw:n_main + 3 * aw]
        past = _past_bias(cache_lft, page_table, l)
        past = past.transpose(0, 2, 1).reshape(bs, n_pages, 1, page * ah)
        ya_s = _fox_sample(u_s[:, n_main:n_main + aw].reshape(ts * ah, HEAD_DIM),
                           k_s.reshape(ts * ah, HEAD_DIM), v_s.reshape(ts * ah, HEAD_DIM),
                           lf_s.reshape(ts * ah, 1), lf_s.reshape(bs, 1, ls * ah),
                           cache_k2, cache_v2, past, page_table, l, ah).reshape(ts, aw)

        x1_p, h2_p, h2slab_p, sco_p = _outproj(yc_p, ym_p, ya_p, xp, w_out_b, gqm, gt1p, gpf, sc2p, sh2p,
                                               w_router_t, 256)
        x1_s, h2_s, h2slab_s, sco_s = _outproj(yc_s, ym_s, ya_s, xs, w_out_b, gqm, gt1s, gpf, sc2s, sh2s,
                                               w_router_t, tm_s)

        idx_t, gw_t, rank_t, counts = _route(jnp.concatenate([sco_p, sco_s], axis=1), router_bias[l])
        row_tok, pos, block_e, n_active = _dispatch(idx_t, rank_t, counts)
        h2_all = jnp.concatenate([h2slab_p, h2slab_s], axis=0)
        ysort = _experts(h2_all, row_tok, block_e, n_active, w_exp_gate, w_exp_up, w_exp_down, l)
        gw = gw_t.T
        xp = _combine(ysort, pos[:tp], x1_p, h2_p, gw[:tp], wsg, wsu, wsd, gqf, gt2p, 128)
        xs = _combine(ysort, pos[tp:], x1_s, h2_s, gw[tp:], wsg, wsu, wsd, gqf, gt2s, min(128, ts))

        outs["kp"].append(u_p[:, n_main + aw:n_main + 2 * aw].reshape(bp, lp, ah, HEAD_DIM))
        outs["vp"].append(u_p[:, n_main + 2 * aw:n_main + 3 * aw].reshape(bp, lp, ah, HEAD_DIM))
        outs["lfp"].append(lf_p)
        outs["ks"].append(k_s.reshape(bs, ls, ah, HEAD_DIM))
        outs["vs"].append(v_s.reshape(bs, ls, ah, HEAD_DIM))
        outs["lfs"].append(lf_s)
        outs["convp"].append(conv_p)
        outs["convs"].append(conv_s)
        outs["cp"].append(c_p)
        outs["np"].append(n_p)
        outs["mp"].append(m_p.reshape(bp, mh))
        outs["cs"].append(c_s)
        outs["ns"].append(n_s)
        outs["ms"].append(m_s.reshape(bs, mh))

    st = {k: jnp.stack(v) for k, v in outs.items()}
    return (xp.reshape(bp, lp, d), xs.reshape(bs, ls, d),
            st["kp"], st["vp"], st["lfp"], st["ks"], st["vs"], st["lfs"],
            st["convp"], st["convs"], st["cp"], st["np"], st["mp"], st["cs"], st["ns"], st["ms"])
```

```python
import functools

import jax
import jax.numpy as jnp
from jax import lax
from jax.experimental import pallas as pl
from jax.experimental.pallas import tpu as pltpu

F32 = jnp.float32
BF16 = jnp.bfloat16
I32 = jnp.int32
HI = lax.Precision.HIGHEST

HEAD_DIM = 128
CONV_W = 31
MLSTM_CHUNK = 64
N_EXPERT_GROUPS = 8
TOPK_GROUPS = 4
TOP_K = 8
ROUTED_SCALE = 2.5
NORM_EPS = 1e-6

LANES = 128
GATE_COLS = 128
EXPERT_BLOCK = 256
VMEM_LIMIT = 56 * 1024 * 1024
WEIGHT_DMA_PRIORITY = 1


def _params(sem):
    return pltpu.CompilerParams(dimension_semantics=sem, vmem_limit_bytes=VMEM_LIMIT)


def _log_sigmoid(x):
    return jnp.minimum(x, 0.0) - jnp.log1p(jnp.exp(-jnp.abs(x)))


def _silu(x):
    return x * jax.nn.sigmoid(x)


def _rms(x):
    return x * lax.rsqrt(jnp.mean(x * x, axis=-1, keepdims=True) + NORM_EPS)


def _dot(a, b):
    return jnp.dot(a, b, preferred_element_type=F32)


def _dot_nt(a, b):
    return lax.dot_general(a, b, (((1,), (1,)), ((), ())), preferred_element_type=F32)


def _dot_tn(a, b):
    return lax.dot_general(a, b, (((0,), (0,)), ((), ())), preferred_element_type=F32)


def _dot_hi(a, b):
    return jnp.dot(a, b, preferred_element_type=F32, precision=HI)


def _split3(a):
    hi = a.astype(BF16)
    rest = a - hi.astype(F32)
    mid = rest.astype(BF16)
    lo = (rest - mid.astype(F32)).astype(BF16)
    return hi, mid, lo


def _dot_ones(a, ones):
    ob = ones.astype(BF16)
    hi, mid, lo = _split3(a)
    return _dot(hi, ob) + _dot(mid, ob) + _dot(lo, ob)


def _ones_dot(ones, a):
    ob = ones.astype(BF16)
    hi, mid, lo = _split3(a)
    return _dot(ob, hi) + _dot(ob, mid) + _dot(ob, lo)


def _lane_tile(t, cap):
    best = LANES
    for k in range(1, cap // LANES + 1):
        if t % (k * LANES) == 0:
            best = k * LANES
    return best


def _store_slabs(ref, x):
    rows, d = x.shape
    nslab = d // LANES
    for j in range(nslab):
        ref[pl.ds(j, rows, stride=nslab), :] = x[:, j * LANES:(j + 1) * LANES]


def _load_slabs(ref, rows, nslab, lead=()):
    return jnp.concatenate([ref[lead + (pl.ds(j, rows, stride=nslab), slice(None))] for j in range(nslab)],
                           axis=1)


def _ada_kernel(c_ref, w_ref, b_ref, o_ref):
    s = _silu(c_ref[...]).astype(BF16)
    o_ref[0] = _dot(s, w_ref[0].astype(BF16)) + b_ref[0]


def _ada(c_all, w_ada, b_ada):
    depth, d, n = w_ada.shape
    r = c_all.shape[0]
    tn = 1024
    return pl.pallas_call(
        _ada_kernel,
        grid=(depth, n // tn),
        in_specs=[pl.BlockSpec((r, d), lambda l, j: (0, 0)),
                  pl.BlockSpec((1, d, tn), lambda l, j: (l, 0, j)),
                  pl.BlockSpec((1, 1, tn), lambda l, j: (l, 0, j))],
        out_specs=pl.BlockSpec((1, r, tn), lambda l, j: (l, 0, j)),
        out_shape=jax.ShapeDtypeStruct((depth, r, n), F32),
        compiler_params=_params(("arbitrary", "arbitrary")),
        name="ada",
    )(c_all, w_ada, b_ada.reshape(depth, 1, n))


def _mod_spec(mod, t, tm, row_axis):
    ngrp, r, d = mod.shape
    if r == 1:
        tiles_per_grp = (t // ngrp) // tm
        return pl.BlockSpec((1, 1, d), lambda *ix: (ix[row_axis] // tiles_per_grp, 0, 0))
    return pl.BlockSpec((1, tm, d), lambda *ix: (0, ix[row_axis], 0))


def _inproj_kernel(x_ref, g_ref, sc_ref, sh_ref, w_ref, o_ref):
    h = (_rms(x_ref[...]) * g_ref[...]) * (1.0 + sc_ref[0]) + sh_ref[0]
    o_ref[...] = _dot(h.astype(BF16), w_ref[...])


def _inproj(x, g, sc, sh, w, tm, tn):
    t, d = x.shape
    n = w.shape[1]
    mod_spec = _mod_spec(sc, t, tm, 1)
    return pl.pallas_call(
        _inproj_kernel,
        grid=(n // tn, t // tm),
        in_specs=[pl.BlockSpec((tm, d), lambda j, i: (i, 0)),
                  pl.BlockSpec((1, d), lambda j, i: (0, 0)),
                  mod_spec, mod_spec,
                  pl.BlockSpec((d, tn), lambda j, i: (0, j))],
        out_specs=pl.BlockSpec((tm, tn), lambda j, i: (i, j)),
        out_shape=jax.ShapeDtypeStruct((t, n), F32),
        compiler_params=_params(("arbitrary", "arbitrary")),
        name="inproj",
    )(x, g, sc, sh, w)


def _conv_kernel(a_ref, g_ref, buf_ref, dw_ref, db_ref, lg_ref, lb_ref, y_ref, st_ref, full_ref,
                 *, lc, nchunks):
    halo = CONV_W + 1
    j = pl.program_id(1)

    @pl.when(j == 0)
    def _():
        full_ref[0:2, :] = jnp.zeros((2, full_ref.shape[1]), F32)
        full_ref[2:halo, :] = buf_ref[0]

    u = a_ref[...] * jax.nn.sigmoid(g_ref[...])
    full_ref[halo:halo + lc, :] = u
    acc = jnp.zeros_like(u)
    for w in range(CONV_W):
        acc = acc + full_ref[2 + w:2 + w + lc, :] * dw_ref[w:w + 1, :]
    acc = acc + db_ref[...]
    mu = jnp.mean(acc, axis=-1, keepdims=True)
    cen = acc - mu
    var = jnp.mean(cen * cen, axis=-1, keepdims=True)
    y = cen * lax.rsqrt(var + NORM_EPS) * lg_ref[...] + lb_ref[...]
    y_ref[...] = _silu(y)

    @pl.when(j == nchunks - 1)
    def _():
        st_ref[0] = full_ref[2 + lc:halo + lc, :]

    if nchunks > 1:
        full_ref[0:halo, :] = full_ref[lc:lc + halo, :]


def _conv(u, buf, dw, db, lg, lb, nb, seq, lc):
    c = buf.shape[2]
    nchunks = seq // lc
    vec = pl.BlockSpec((1, c), lambda b, j: (0, 0))
    return pl.pallas_call(
        functools.partial(_conv_kernel, lc=lc, nchunks=nchunks),
        grid=(nb, nchunks),
        in_specs=[pl.BlockSpec((lc, c), lambda b, j: (b * nchunks + j, 0)),
                  pl.BlockSpec((lc, c), lambda b, j: (b * nchunks + j, 1)),
                  pl.BlockSpec((1, CONV_W - 1, c), lambda b, j: (b, 0, 0)),
                  pl.BlockSpec((CONV_W, c), lambda b, j: (0, 0)),
                  vec, vec, vec],
        out_specs=[pl.BlockSpec((lc, c), lambda b, j: (b * nchunks + j, 0)),
                   pl.BlockSpec((1, CONV_W - 1, c), lambda b, j: (b, 0, 0))],
        out_shape=[jax.ShapeDtypeStruct((nb * seq, c), F32),
                   jax.ShapeDtypeStruct((nb, CONV_W - 1, c), F32)],
        scratch_shapes=[pltpu.VMEM((CONV_W + 1 + lc, c), F32)],
        compiler_params=_params(("arbitrary", "arbitrary")),
        name="conv",
    )(u, u, buf, dw, db, lg, lb)


def _mlstm_kernel(q_ref, k_ref, v_ref, o_ref, gc_ref, gr_ref, bc_ref, br_ref, c0_ref, n0_ref, m0_ref,
                  y_ref, cout_ref, nout_ref, mout_ref, c_s, n_s, m_s, *, c, nc, nh, nbs):
    j = pl.program_id(1)

    @pl.when(j == 0)
    def _():
        c_s[...] = c0_ref[...]
        n_s[...] = n0_ref[...]
        for b in range(nbs):
            for h in range(nh):
                m_s[b, h] = jnp.broadcast_to(m0_ref[b][:, h:h + 1], m_s.shape[2:])

    for b in range(nbs):
        _mlstm_chunk(b, q_ref, k_ref, v_ref, o_ref, gc_ref, gr_ref, bc_ref, br_ref, y_ref, c_s, n_s, m_s,
                     c=c, nh=nh)

    @pl.when(j == nc - 1)
    def _():
        cout_ref[...] = c_s[...]
        nout_ref[...] = n_s[...]
        lane = lax.broadcasted_iota(I32, (1, nh), 1)
        for b in range(nbs):
            m_out = jnp.zeros((1, nh), F32)
            for h in range(nh):
                m_out = jnp.where(lane == h, m_s[b, h][0:1, 0:1], m_out)
            mout_ref[b] = m_out


def _mlstm_chunk(b, q_ref, k_ref, v_ref, o_ref, gc_ref, gr_ref, bc_ref, br_ref, y_ref, c_s, n_s, m_s, *, c, nh):
    scale = HEAD_DIM ** -0.5
    mm = BF16 if c % 16 == 0 else F32
    gcol = gc_ref[b] + bc_ref[...]
    grow = gr_ref[b, 0] + br_ref[...]
    lf_col = _log_sigmoid(gcol)
    lf_row = _log_sigmoid(grow)
    row = lax.broadcasted_iota(I32, (c, c), 0)
    col = lax.broadcasted_iota(I32, (c, c), 1)
    causal = col <= row
    if mm == BF16:
        b_col_all = _ones_dot(causal, lf_col)
        b_row_all = _dot_ones(lf_row, row <= col)
    else:
        b_col_all = _dot_hi(causal.astype(F32), lf_col)
        b_row_all = _dot_hi(lf_row, (row <= col).astype(F32))

    for h in range(nh):
        sl = slice(h * HEAD_DIM, (h + 1) * HEAD_DIM)
        q = q_ref[b, :, sl]
        k = k_ref[b, :, sl] * scale
        v = v_ref[b, :, sl]
        b_c = b_col_all[:, nh + h:nh + h + 1]
        b_r = b_row_all[nh + h:nh + h + 1, :]
        ig_r = grow[h:h + 1, :]
        ig_c = gcol[:, h:h + 1]
        m_prev = m_s[b, h][0:1, 0:1]
        dmat = jnp.where(causal, b_c - b_r + ig_r, -jnp.inf)
        inter = b_c + m_prev
        m_t = jnp.maximum(inter, jnp.max(dmat, axis=1, keepdims=True))
        dw = jnp.exp(dmat - m_t)
        iw = jnp.exp(inter - m_t)
        qm, km, vm = q.astype(mm), k.astype(mm), v.astype(mm)
        s = _dot_nt(qm, km) * dw
        c_h = c_s[b, h]
        n_h = n_s[b, h:h + 1, :]
        num = _dot(s.astype(mm), vm) + iw * _dot(qm, c_h.astype(mm))
        den = jnp.sum(s, axis=1, keepdims=True) + iw * jnp.sum(q * n_h, axis=1, keepdims=True)
        hid = num / jnp.maximum(jnp.abs(den), jnp.exp(-m_t))
        y_ref[b, :, sl] = jax.nn.sigmoid(o_ref[b, :, sl]) * hid
        m_last = m_t[c - 1:c, :]
        w_end = jnp.exp(b_c[c - 1:c, :] - b_c + ig_c - m_last)
        f_end = iw[c - 1:c, :]
        kw = k * w_end
        c_s[b, h] = f_end * c_h + _dot_tn(kw.astype(mm), vm)
        n_s[b, h:h + 1, :] = f_end * n_h + jnp.sum(kw, axis=0, keepdims=True)
        m_s[b, h] = jnp.broadcast_to(m_last, m_s.shape[2:])


def _mlstm(u, gates, gates_t, bias_row, bias_col, c0, n0, m0, nb, seq, col0):
    nh = c0.shape[1]
    w = nh * HEAD_DIM
    c = MLSTM_CHUNK if seq % MLSTM_CHUNK == 0 else seq
    nc = seq // c
    nbs = 4 if nb % 4 == 0 else 1
    u3 = u.reshape(nb, seq, u.shape[1])

    def ublk(off):
        return pl.BlockSpec((nbs, c, w), lambda g, j: (g, j, col0 + off))

    y, c_out, n_out, m_out = pl.pallas_call(
        functools.partial(_mlstm_kernel, c=c, nc=nc, nh=nh, nbs=nbs),
        grid=(nb // nbs, nc),
        in_specs=[ublk(0), ublk(1), ublk(2), ublk(3),
                  pl.BlockSpec((nbs, c, GATE_COLS), lambda g, j: (g, j, 0)),
                  pl.BlockSpec((nbs, 1, 2 * nh, c), lambda g, j: (g, j, 0, 0)),
                  pl.BlockSpec((1, GATE_COLS), lambda g, j: (0, 0)),
                  pl.BlockSpec((2 * nh, 1), lambda g, j: (0, 0)),
                  pl.BlockSpec((nbs, nh, HEAD_DIM, HEAD_DIM), lambda g, j: (g, 0, 0, 0)),
                  pl.BlockSpec((nbs, nh, HEAD_DIM), lambda g, j: (g, 0, 0)),
                  pl.BlockSpec((nbs, 1, nh), lambda g, j: (g, 0, 0))],
        out_specs=[pl.BlockSpec((nbs, c, w), lambda g, j: (g, j, 0)),
                   pl.BlockSpec((nbs, nh, HEAD_DIM, HEAD_DIM), lambda g, j: (g, 0, 0, 0)),
                   pl.BlockSpec((nbs, nh, HEAD_DIM), lambda g, j: (g, 0, 0)),
                   pl.BlockSpec((nbs, 1, nh), lambda g, j: (g, 0, 0))],
        out_shape=[jax.ShapeDtypeStruct((nb, seq, w), F32),
                   jax.ShapeDtypeStruct((nb, nh, HEAD_DIM, HEAD_DIM), F32),
                   jax.ShapeDtypeStruct((nb, nh, HEAD_DIM), F32),
                   jax.ShapeDtypeStruct((nb, 1, nh), F32)],
        scratch_shapes=[pltpu.VMEM((nbs, nh, HEAD_DIM, HEAD_DIM), F32),
                        pltpu.VMEM((nbs, nh, HEAD_DIM), F32),
                        pltpu.VMEM((nbs, nh, 8, LANES), F32)],
        compiler_params=_params(("arbitrary", "arbitrary")),
        name="mlstm",
    )(u3, u3, u3, u3, gates.reshape(nb, seq, GATE_COLS), gates_t, bias_row, bias_col, c0, n0,
      m0.reshape(nb, 1, nh))
    return y.reshape(nb * seq, w), c_out, n_out, m_out


def _fox_prompt_kernel(q_ref, k_ref, v_ref, fc_ref, fr_ref, o_ref, *, seq, tq):
    scale = HEAD_DIM ** -0.5
    kb = k_ref[...].astype(BF16)
    vb = v_ref[...].astype(BF16)
    f_row = fr_ref[0, 0]
    for qi in range(seq // tq):
        rows = slice(qi * tq, (qi + 1) * tq)
        kend = (qi + 1) * tq
        q = (q_ref[rows, :] * scale).astype(BF16)
        s = _dot_nt(q, kb[:kend]) + fc_ref[0, 0][rows, :] - f_row[:, :kend]
        qpos = lax.broadcasted_iota(I32, (tq, kend), 0) + qi * tq
        kpos = lax.broadcasted_iota(I32, (tq, kend), 1)
        s = jnp.where(kpos <= qpos, s, -jnp.inf)
        p = jnp.exp(s - jnp.max(s, axis=1, keepdims=True))
        denom = jnp.sum(p, axis=1, keepdims=True)
        o_ref[rows, :] = _dot(p.astype(BF16), vb[:kend]) / denom


def _fox_prompt(u, f_col, f_row, nb, seq, nh, qcol, kcol, vcol):
    tq = min(256, seq)
    return pl.pallas_call(
        functools.partial(_fox_prompt_kernel, seq=seq, tq=tq),
        grid=(nb, nh),
        in_specs=[pl.BlockSpec((seq, HEAD_DIM), lambda b, h: (b, qcol + h)),
                  pl.BlockSpec((seq, HEAD_DIM), lambda b, h: (b, kcol + h)),
                  pl.BlockSpec((seq, HEAD_DIM), lambda b, h: (b, vcol + h)),
                  pl.BlockSpec((1, 1, seq, 1), lambda b, h: (b, h, 0, 0)),
                  pl.BlockSpec((1, 1, 1, seq), lambda b, h: (b, h, 0, 0))],
        out_specs=pl.BlockSpec((seq, HEAD_DIM), lambda b, h: (b, h)),
        out_shape=jax.ShapeDtypeStruct((nb * seq, nh * HEAD_DIM), F32),
        compiler_params=_params(("arbitrary", "arbitrary")),
        name="fox_prompt",
    )(u, u, u, f_col, f_row)


def _pages_per_step(n_pages):
    return 8 if n_pages % 8 == 0 else 1


def _past_bias_kernel(pt_ref, *rest, npg):
    lf_refs = rest[:npg]
    o_ref, carry_s = rest[npg:]
    page = lf_refs[0].shape[3]

    @pl.when(pl.program_id(1) == 0)
    def _():
        carry_s[...] = jnp.zeros(carry_s.shape, F32)

    later = lax.broadcasted_iota(I32, (page, page), 0) > lax.broadcasted_iota(I32, (page, page), 1)
    carry = carry_s[...]
    for r in reversed(range(npg)):
        lft = lf_refs[r][0, 0]
        o_ref[0, :, r * page:(r + 1) * page] = carry + _dot_ones(lft, later)
        carry = carry + jnp.sum(lft, axis=1, keepdims=True)
    carry_s[...] = carry


def _past_bias(cache_lft, page_table, layer):
    nb, n_pages = page_table.shape
    nh, page = cache_lft.shape[2], cache_lft.shape[3]
    npg = 32 if n_pages % 32 == 0 else _pages_per_step(n_pages)
    ngrp = n_pages // npg
    specs = [pl.BlockSpec((1, 1, nh, page),
                          lambda b, g, pt, r=r: (layer, pt[b, (ngrp - 1 - g) * npg + r], 0, 0))
             for r in range(npg)]
    grid_spec = pltpu.PrefetchScalarGridSpec(
        num_scalar_prefetch=1,
        grid=(nb, ngrp),
        in_specs=specs,
        out_specs=pl.BlockSpec((1, nh, npg * page), lambda b, g, pt: (b, 0, ngrp - 1 - g)),
        scratch_shapes=[pltpu.VMEM((nh, 1), F32)],
    )
    return pl.pallas_call(
        functools.partial(_past_bias_kernel, npg=npg),
        grid_spec=grid_spec,
        out_shape=jax.ShapeDtypeStruct((nb, nh, n_pages * page), F32),
        compiler_params=_params(("arbitrary", "arbitrary")),
        name="fox_past_bias",
    )(page_table, *([cache_lft] * npg))


def _fox_sample_kernel(pt_ref, q_ref, kn_ref, vn_ref, lfc_ref, lfr_ref, *rest, npg, ngrp, nh):
    k_refs = rest[0:npg]
    v_refs = rest[npg:2 * npg]
    b_refs = rest[2 * npg:3 * npg]
    o_ref, m_s, l_s, acc_s = rest[3 * npg:]
    g = pl.program_id(1)
    scale = HEAD_DIM ** -0.5
    nr = q_ref.shape[0]
    ncol = k_refs[0].shape[2]
    hshift = nh.bit_length() - 1
    qb = (q_ref[...] * scale).astype(BF16)

    ri = lax.broadcasted_iota(I32, (nr, nr), 0)
    ci = lax.broadcasted_iota(I32, (nr, nr), 1)
    same_head = (ri & (nh - 1)) == (ci & (nh - 1))
    col_le_row = jnp.logical_and(same_head, (ci >> hshift) <= (ri >> hshift))
    row_le_col = jnp.logical_and(same_head, (ri >> hshift) <= (ci >> hshift))
    f_col = jnp.sum(jnp.where(col_le_row, lfr_ref[0], 0.0), axis=1, keepdims=True)

    @pl.when(g == 0)
    def _():
        f_row = jnp.sum(jnp.where(row_le_col, lfc_ref[...], 0.0), axis=0, keepdims=True)
        s = _dot_nt(qb, kn_ref[...].astype(BF16)) + f_col - f_row
        s = jnp.where(col_le_row, s, -jnp.inf)
        m = jnp.max(s, axis=1, keepdims=True)
        p = jnp.exp(s - m)
        m_s[...] = m
        l_s[...] = jnp.sum(p, axis=1, keepdims=True)
        acc_s[...] = _dot(p.astype(BF16), vn_ref[...].astype(BF16))

    rr = lax.broadcasted_iota(I32, (nr, ncol), 0)
    cc = lax.broadcasted_iota(I32, (nr, ncol), 1)
    base = f_col + jnp.where((rr & (nh - 1)) == (cc & (nh - 1)), 0.0, -jnp.inf)
    s = jnp.concatenate([_dot_nt(qb, k_refs[r][0, 0].astype(BF16)) + (base + b_refs[r][0, 0])
                         for r in range(npg)], axis=1)
    m_old = m_s[...]
    m_new = jnp.maximum(m_old, jnp.max(s, axis=1, keepdims=True))
    alpha = jnp.exp(m_old - m_new)
    p = jnp.exp(s - m_new)
    l_s[...] = alpha * l_s[...] + jnp.sum(p, axis=1, keepdims=True)
    pv = _dot(p[:, 0:ncol].astype(BF16), v_refs[0][0, 0].astype(BF16))
    for r in range(1, npg):
        pv = pv + _dot(p[:, r * ncol:(r + 1) * ncol].astype(BF16), v_refs[r][0, 0].astype(BF16))
    acc_s[...] = alpha * acc_s[...] + pv
    m_s[...] = m_new

    @pl.when(g == ngrp - 1)
    def _():
        o_ref[...] = acc_s[...] / l_s[...]


def _fox_sample(q2, kn2, vn2, lf_col, lf_row, cache_k2, cache_v2, bias, page_table, layer, nh):
    nb, n_pages = page_table.shape
    nr = q2.shape[0] // nb
    ncol = cache_k2.shape[2]
    npg = _pages_per_step(n_pages)
    ngrp = n_pages // npg
    assert nh & (nh - 1) == 0

    kv_specs = [pl.BlockSpec((1, 1, ncol, HEAD_DIM), lambda b, g, pt, r=r: (layer, pt[b, g * npg + r], 0, 0))
                for r in range(npg)]
    b_specs = [pl.BlockSpec((1, 1, 1, ncol), lambda b, g, pt, r=r: (b, g * npg + r, 0, 0)) for r in range(npg)]
    rows = pl.BlockSpec((nr, HEAD_DIM), lambda b, g, pt: (b, 0))
    grid_spec = pltpu.PrefetchScalarGridSpec(
        num_scalar_prefetch=1,
        grid=(nb, ngrp),
        in_specs=[rows, rows, rows,
                  pl.BlockSpec((nr, 1), lambda b, g, pt: (b, 0)),
                  pl.BlockSpec((1, 1, nr), lambda b, g, pt: (b, 0, 0))] + kv_specs + kv_specs + b_specs,
        out_specs=rows,
        scratch_shapes=[pltpu.VMEM((nr, 1), F32), pltpu.VMEM((nr, 1), F32), pltpu.VMEM((nr, HEAD_DIM), F32)],
    )
    return pl.pallas_call(
        functools.partial(_fox_sample_kernel, npg=npg, ngrp=ngrp, nh=nh),
        grid_spec=grid_spec,
        out_shape=jax.ShapeDtypeStruct(q2.shape, F32),
        compiler_params=_params(("arbitrary", "arbitrary")),
        name="fox_sample",
    )(page_table, q2, kn2, vn2, lf_col, lf_row, *([cache_k2] * npg), *([cache_v2] * npg), *([bias] * npg))


def _outproj_kernel(yc_ref, ym_ref, ya_ref, x_ref, wc_ref, wm_ref, wa_ref, gpost_ref, gt_ref, gpre_ref,
                    sc_ref, sh_ref, wrt_ref, x1_ref, h2_ref, h2s_ref, sco_ref):
    mix = (_dot(yc_ref[...].astype(BF16), wc_ref[...]) + _dot(ym_ref[...].astype(BF16), wm_ref[...])
           + _dot(ya_ref[...].astype(BF16), wa_ref[...]))
    x1 = x_ref[...] + gt_ref[0] * (_rms(mix) * gpost_ref[...])
    x1_ref[...] = x1
    h2 = (_rms(x1) * gpre_ref[...]) * (1.0 + sc_ref[0]) + sh_ref[0]
    h2_ref[...] = h2
    _store_slabs(h2s_ref, h2)
    w_hi, w_lo, _ = _split3(wrt_ref[...])
    h_hi, h_lo, _ = _split3(h2)
    logits_t = _dot_nt(w_hi, h_hi) + (_dot_nt(w_hi, h_lo) + _dot_nt(w_lo, h_hi))
    sco_ref[...] = jax.nn.sigmoid(logits_t)


def _outproj(yc, ym, ya, x, w_out, gpost, gt, gpre, sc, sh, w_router_t, tm):
    t, d = x.shape
    cw, mw, aw = yc.shape[1], ym.shape[1], ya.shape[1]
    ne = w_router_t.shape[0]
    mod_spec = _mod_spec(sc, t, tm, 0)
    vec = pl.BlockSpec((1, d), lambda i: (0, 0))
    return pl.pallas_call(
        _outproj_kernel,
        grid=(t // tm,),
        in_specs=[pl.BlockSpec((tm, cw), lambda i: (i, 0)),
                  pl.BlockSpec((tm, mw), lambda i: (i, 0)),
                  pl.BlockSpec((tm, aw), lambda i: (i, 0)),
                  pl.BlockSpec((tm, d), lambda i: (i, 0)),
                  pl.BlockSpec((cw, d), lambda i: (0, 0)),
                  pl.BlockSpec((mw, d), lambda i: (cw // mw, 0)),
                  pl.BlockSpec((aw, d), lambda i: ((cw + mw) // aw, 0)),
                  vec, mod_spec, vec, mod_spec, mod_spec,
                  pl.BlockSpec((ne, d), lambda i: (0, 0))],
        out_specs=[pl.BlockSpec((tm, d), lambda i: (i, 0)),
                   pl.BlockSpec((tm, d), lambda i: (i, 0)),
                   pl.BlockSpec((tm * (d // LANES), LANES), lambda i: (i, 0)),
                   pl.BlockSpec((ne, tm), lambda i: (0, i))],
        out_shape=[jax.ShapeDtypeStruct((t, d), F32),
                   jax.ShapeDtypeStruct((t, d), F32),
                   jax.ShapeDtypeStruct((t * (d // LANES), LANES), F32),
                   jax.ShapeDtypeStruct((ne, t), F32)],
        compiler_params=_params(("arbitrary",)),
        name="outproj",
    )(yc, ym, ya, x, w_out, w_out, w_out, gpost, gt, gpre, sc, sh, w_router_t)


def _route_kernel(s_ref, b_ref, idx_ref, w_ref, rank_ref, cnt_ref, carry_s, *, ne, ng, tt, ntiles):
    i = pl.program_id(0)
    gs = ne // ng

    @pl.when(i == 0)
    def _():
        carry_s[...] = jnp.zeros(carry_s.shape, F32)

    def first_max(x, ids, n):
        m = jnp.max(x, axis=0, keepdims=True)
        return m, jnp.min(jnp.where(x == m, ids, float(n)), axis=0, keepdims=True)

    s = s_ref[...]
    sel = s + b_ref[...]
    sub = lax.broadcasted_iota(I32, (gs, tt), 0).astype(F32)
    grow = lax.broadcasted_iota(I32, (ng, tt), 0).astype(F32)
    gscore = jnp.zeros((ng, tt), F32)
    for g in range(ng):
        blk = sel[g * gs:(g + 1) * gs, :]
        m1, i1 = first_max(blk, sub, gs)
        m2 = jnp.max(jnp.where(sub == i1, -jnp.inf, blk), axis=0, keepdims=True)
        gscore = jnp.where(grow == g, m1 + m2, gscore)
    gsel = jnp.zeros((ng, tt), F32)
    for _ in range(TOPK_GROUPS):
        _, ig = first_max(gscore, grow, ng)
        hit = grow == ig
        gsel = jnp.where(hit, 1.0, gsel)
        gscore = jnp.where(hit, -jnp.inf, gscore)
    cand = jnp.concatenate([jnp.where(gsel[g:g + 1, :] > 0.0, sel[g * gs:(g + 1) * gs, :], -jnp.inf)
                            for g in range(ng)], axis=0)

    erow = lax.broadcasted_iota(I32, (ne, tt), 0).astype(F32)
    member = jnp.zeros((ne, tt), F32)
    picks, weights = [], []
    for _ in range(TOP_K):
        _, ie = first_max(cand, erow, ne)
        hit = erow == ie
        picks.append(ie)
        weights.append(jnp.sum(jnp.where(hit, s, 0.0), axis=0, keepdims=True))
        member = jnp.where(hit, 1.0, member)
        cand = jnp.where(hit, -jnp.inf, cand)
    wsum = weights[0]
    for k in range(1, TOP_K):
        wsum = wsum + weights[k]

    upto = lax.broadcasted_iota(I32, (tt, tt), 0) <= lax.broadcasted_iota(I32, (tt, tt), 1)
    incl = _dot(member.astype(BF16), upto.astype(BF16)) + carry_s[...]
    rank = incl - member
    carry_s[...] = incl[:, tt - 1:tt]

    krow = lax.broadcasted_iota(I32, (TOP_K, tt), 0)
    idx_out = jnp.zeros((TOP_K, tt), F32)
    w_out = jnp.zeros((TOP_K, tt), F32)
    rank_out = jnp.zeros((TOP_K, tt), F32)
    for k in range(TOP_K):
        idx_out = jnp.where(krow == k, picks[k], idx_out)
        w_out = jnp.where(krow == k, weights[k] / wsum * ROUTED_SCALE, w_out)
        rk = jnp.sum(jnp.where(erow == picks[k], rank, 0.0), axis=0, keepdims=True)
        rank_out = jnp.where(krow == k, rk, rank_out)
    idx_ref[...] = idx_out.astype(I32)
    w_ref[...] = w_out
    rank_ref[...] = rank_out

    @pl.when(i == ntiles - 1)
    def _():
        cnt_ref[...] = carry_s[...]


def _route(scores_t, router_bias):
    ne, t = scores_t.shape
    tt = _lane_tile(t, 768)
    ntiles = t // tt
    tok = pl.BlockSpec((TOP_K, tt), lambda i: (0, i))
    return pl.pallas_call(
        functools.partial(_route_kernel, ne=ne, ng=N_EXPERT_GROUPS, tt=tt, ntiles=ntiles),
        grid=(ntiles,),
        in_specs=[pl.BlockSpec((ne, tt), lambda i: (0, i)),
                  pl.BlockSpec((ne, 1), lambda i: (0, 0))],
        out_specs=[tok, tok, tok, pl.BlockSpec((ne, 1), lambda i: (0, 0))],
        out_shape=[jax.ShapeDtypeStruct((TOP_K, t), I32),
                   jax.ShapeDtypeStruct((TOP_K, t), F32),
                   jax.ShapeDtypeStruct((TOP_K, t), F32),
                   jax.ShapeDtypeStruct((ne, 1), F32)],
        scratch_shapes=[pltpu.VMEM((ne, 1), F32)],
        compiler_params=_params(("arbitrary",)),
        name="moe_route",
    )(scores_t, router_bias.astype(F32)[:, None])


def _dispatch(idx_t, rank_t, counts):
    topk, t = idx_t.shape
    ne = counts.shape[0]
    a = t * topk
    counts = counts.reshape(ne).astype(I32)
    padded = (counts + EXPERT_BLOCK - 1) // EXPERT_BLOCK * EXPERT_BLOCK
    pends = jnp.cumsum(padded)
    pstarts = pends - padded
    start_of = jnp.sum(jnp.where(idx_t[:, :, None] == jnp.arange(ne, dtype=I32), pstarts, 0), axis=2)
    pos = (start_of + rank_t.astype(I32)).T
    nblk = -(-a // EXPERT_BLOCK) + ne
    tok = jnp.broadcast_to(jnp.arange(t, dtype=I32)[:, None], (t, topk))
    row_tok = jnp.zeros((nblk * EXPERT_BLOCK,), I32).at[pos.reshape(a)].set(tok.reshape(a), unique_indices=True)
    n_active = pends[-1] // EXPERT_BLOCK
    blk = jnp.minimum(jnp.arange(nblk, dtype=I32), n_active - 1)
    block_e = jnp.minimum(jnp.sum((pends[None, :] <= (blk * EXPERT_BLOCK)[:, None]).astype(I32), axis=1), ne - 1)
    return row_tok, pos, block_e, n_active.reshape(1)


def _experts_kernel(be_ref, nact_ref, first_ref, nxt_ref, wsl_ref, idx_ref, idxn_ref, h_ref,
                    wg_hbm, wu_hbm, wd_hbm, y_ref,
                    x_s, wg_f, wu_f, wd_f, wg_s, wu_s, wd_s, xsem, wsem, *, layer):
    i = pl.program_id(0)
    nact = nact_ref[0]
    active = i < nact
    has_next = i + 1 < nact
    slot = lax.rem(i, 2)
    nslab = wg_s.shape[0] // LANES
    nrows = x_s.shape[1] // nslab

    def row_copy(ids_ref, r, sl):
        src = h_ref.at[pl.ds(pl.multiple_of(ids_ref[0, 0, r], nslab), nslab)]
        return pltpu.make_async_copy(src, x_s.at[sl, pl.ds(r * nslab, nslab)], xsem.at[sl])

    def wait_rows(sl):
        pltpu.make_async_copy(h_ref.at[pl.ds(0, nrows * nslab)], x_s.at[sl], xsem.at[sl]).wait()

    def weight_copies(e, sl):
        return (pltpu.make_async_copy(wg_hbm.at[layer, e], wg_f.at[sl], wsem.at[sl, 0]),
                pltpu.make_async_copy(wu_hbm.at[layer, e], wu_f.at[sl], wsem.at[sl, 1]),
                pltpu.make_async_copy(wd_hbm.at[layer, e], wd_f.at[sl], wsem.at[sl, 2]))

    @pl.when(jnp.logical_and(i == 0, active))
    def _():
        def body(r, carry):
            row_copy(idx_ref, r, 0).start()
            return carry
        lax.fori_loop(0, nrows, body, 0)
        for cp in weight_copies(be_ref[0], 0):
            cp.start(priority=WEIGHT_DMA_PRIORITY)

    @pl.when(jnp.logical_and(active, first_ref[i] == 1))
    def _():
        wsl = wsl_ref[i]

        @pl.when(nxt_ref[i] >= 0)
        def _():
            for cp in weight_copies(nxt_ref[i], 1 - wsl):
                cp.start(priority=WEIGHT_DMA_PRIORITY)

        for cp in weight_copies(be_ref[i], wsl):
            cp.wait()
        wg_s[...] = wg_f[wsl].astype(BF16)
        wu_s[...] = wu_f[wsl].astype(BF16)
        wd_s[...] = wd_f[wsl].astype(BF16)

    def compute(sl, prefetch):
        wait_rows(sl)
        x = _load_slabs(x_s, nrows, nslab, lead=(sl,)).astype(BF16)
        if prefetch:
            for r in range(nrows):
                row_copy(idxn_ref, r, 1 - sl).start()
        hid = _silu(_dot(x, wg_s[...])) * _dot(x, wu_s[...])
        _store_slabs(y_ref, _dot(hid.astype(BF16), wd_s[...]))

    for sl in (0, 1):
        @pl.when(jnp.logical_and(has_next, slot == sl))
        def _(sl=sl):
            compute(sl, True)

        @pl.when(jnp.logical_and(jnp.logical_and(active, jnp.logical_not(has_next)), slot == sl))
        def _(sl=sl):
            compute(sl, False)

    @pl.when(jnp.logical_not(active))
    def _():
        y_ref[...] = jnp.zeros(y_ref.shape, F32)


def _experts(h2, row_tok, block_e, n_active, w_gate, w_up, w_down, layer):
    d, de = w_gate.shape[2], w_gate.shape[3]
    nslab = d // LANES
    nblk = row_tok.shape[0] // EXPERT_BLOCK
    ids = (row_tok * nslab).reshape(nblk, 1, EXPERT_BLOCK)
    blk = jnp.arange(nblk, dtype=I32)
    first = jnp.logical_and(blk < n_active[0],
                            jnp.concatenate([jnp.ones((1,), bool), block_e[1:] != block_e[:-1]]))
    wslot = lax.rem(jnp.cumsum(first.astype(I32)) - 1, 2).astype(I32)
    first_at = jnp.where(first, blk, nblk)
    next_first = lax.cummin(jnp.concatenate([first_at[1:], jnp.full((1,), nblk, I32)]), reverse=True)
    nxt_e = jnp.where(next_first < nblk, block_e[jnp.minimum(next_first, nblk - 1)], -1).astype(I32)
    any_spec = pl.BlockSpec(memory_space=pl.ANY)
    grid_spec = pltpu.PrefetchScalarGridSpec(
        num_scalar_prefetch=5,
        grid=(nblk,),
        in_specs=[pl.BlockSpec((1, 1, EXPERT_BLOCK), lambda i, *_: (i, 0, 0), memory_space=pltpu.SMEM),
                  pl.BlockSpec((1, 1, EXPERT_BLOCK), lambda i, *_: (jnp.minimum(i + 1, nblk - 1), 0, 0),
                               memory_space=pltpu.SMEM),
                  any_spec, any_spec, any_spec, any_spec],
        out_specs=pl.BlockSpec((EXPERT_BLOCK * nslab, LANES), lambda i, *_: (i, 0)),
        scratch_shapes=[pltpu.VMEM((2, EXPERT_BLOCK * nslab, LANES), F32),
                        pltpu.VMEM((2, d, de), F32), pltpu.VMEM((2, d, de), F32), pltpu.VMEM((2, de, d), F32),
                        pltpu.VMEM((d, de), BF16), pltpu.VMEM((d, de), BF16), pltpu.VMEM((de, d), BF16),
                        pltpu.SemaphoreType.DMA((2,)), pltpu.SemaphoreType.DMA((2, 3))],
    )
    return pl.pallas_call(
        functools.partial(_experts_kernel, layer=layer),
        grid_spec=grid_spec,
        out_shape=jax.ShapeDtypeStruct((nblk * EXPERT_BLOCK * nslab, LANES), F32),
        compiler_params=_params(("arbitrary",)),
        name="moe_experts",
    )(block_e, n_active, first.astype(I32), nxt_e, wslot, ids, ids, h2, w_gate, w_up, w_down)


def _combine_kernel(pos_ref, ys_ref, x1_ref, h2_ref, gw_ref, wsg_ref, wsu_ref, wsd_ref, gpost_ref, gt_ref,
                    o_ref, rows_s, mix_s, sem, *, tt, topk):
    nslab = o_ref.shape[1] // LANES
    for n in range(tt * topk):
        src = ys_ref.at[pl.ds(pl.multiple_of(pos_ref[0, 0, n], nslab), nslab)]
        pltpu.make_async_copy(src, rows_s.at[pl.ds(n * nslab, nslab)], sem).start(priority=n % 2)
    h2 = h2_ref[...].astype(BF16)
    shared = _dot((_silu(_dot(h2, wsg_ref[...])) * _dot(h2, wsu_ref[...])).astype(BF16), wsd_ref[...])
    pltpu.make_async_copy(ys_ref.at[pl.ds(0, tt * topk * nslab)], rows_s, sem).wait()
    for t in range(tt):
        acc = None
        for k in range(topk):
            n = t * topk + k
            term = gw_ref[0, 0, n] * rows_s[n * nslab:(n + 1) * nslab, :]
            acc = term if acc is None else acc + term
        mix_s[t * nslab:(t + 1) * nslab, :] = acc
    f = shared + _load_slabs(mix_s, tt, nslab)
    o_ref[...] = x1_ref[...] + gt_ref[0] * (_rms(f) * gpost_ref[...])


def _combine(ys, pos, x1, h2, gw, wsg, wsu, wsd, gpost, gt, tt):
    t, d = x1.shape
    topk = gw.shape[1]
    de = wsg.shape[1]
    nslab = d // LANES
    scalars = pl.BlockSpec((1, 1, tt * topk), lambda i: (i, 0, 0), memory_space=pltpu.SMEM)
    return pl.pallas_call(
        functools.partial(_combine_kernel, tt=tt, topk=topk),
        grid=(t // tt,),
        in_specs=[scalars,
                  pl.BlockSpec(memory_space=pl.ANY),
                  pl.BlockSpec((tt, d), lambda i: (i, 0)),
                  pl.BlockSpec((tt, d), lambda i: (i, 0)),
                  scalars,
                  pl.BlockSpec((d, de), lambda i: (0, 0)),
                  pl.BlockSpec((d, de), lambda i: (0, 0)),
                  pl.BlockSpec((de, d), lambda i: (0, 0)),
                  pl.BlockSpec((1, d), lambda i: (0, 0)),
                  _mod_spec(gt, t, tt, 0)],
        out_specs=pl.BlockSpec((tt, d), lambda i: (i, 0)),
        out_shape=jax.ShapeDtypeStruct((t, d), F32),
        scratch_shapes=[pltpu.VMEM((tt * topk * nslab, LANES), F32), pltpu.VMEM((tt * nslab, LANES), F32),
                        pltpu.SemaphoreType.DMA(())],
        compiler_params=_params(("arbitrary",)),
        name="moe_combine",
    )((pos * nslab).reshape(t // tt, 1, tt * topk), ys, x1, h2, gw.reshape(t // tt, 1, tt * topk),
      wsg, wsu, wsd, gpost, gt)


def _mod_parts(mod, nb_p, reps):
    parts = jnp.split(mod, 6, axis=-1)
    prompt = [m[:nb_p, None, :] for m in parts]
    sample = [jnp.repeat(m[nb_p:], reps, axis=0)[None] for m in parts]
    return prompt, sample


def kernel(x_prompt, x_sample, cache_k, cache_v, cache_logf, state_conv, state_mlstm_C, state_mlstm_n,
           state_mlstm_m, page_table, c_prompt, c_sample, w_ada, b_ada, g_pre_mix, g_post_mix, g_pre_ffn,
           g_post_ffn, w_in, w_out, conv_dw, conv_db, conv_ln_g, conv_ln_b, mlstm_i_bias, mlstm_f_bias,
           fox_f_bias, w_router, router_bias, w_exp_gate, w_exp_up, w_exp_down, w_sh_gate, w_sh_up, w_sh_down):
    bp, lp, d = x_prompt.shape
    bs, ls, _ = x_sample.shape
    depth = w_ada.shape[0]
    cc = conv_dw.shape[2]
    mh = state_mlstm_C.shape[2]
    mw = mh * HEAD_DIM
    ah = cache_k.shape[3]
    aw = ah * HEAD_DIM
    n_phys, page = cache_k.shape[1], cache_k.shape[2]
    n_pages = page_table.shape[1]
    tp, ts = bp * lp, bs * ls
    n_main = 2 * cc + 4 * mw
    g0 = n_main
    a0 = g0 + 2 * mh
    f0 = a0 + 3 * aw

    xp = x_prompt.reshape(tp, d)
    xs = x_sample.reshape(ts, d)
    mod_all = _ada(jnp.concatenate([c_prompt, c_sample], axis=0), w_ada, b_ada)
    cache_k2 = cache_k.reshape(depth, n_phys, page * ah, HEAD_DIM)
    cache_v2 = cache_v.reshape(depth, n_phys, page * ah, HEAD_DIM)
    cache_lft = jnp.swapaxes(cache_logf, 2, 3)
    zero_conv = jnp.zeros((bp, CONV_W - 1, cc), F32)
    zero_c = jnp.zeros((bp, mh, HEAD_DIM, HEAD_DIM), F32)
    zero_n = jnp.zeros((bp, mh, HEAD_DIM), F32)
    zero_m = jnp.zeros((bp, mh), F32)
    tm_s = ts

    outs = {k: [] for k in ("kp", "vp", "lfp", "ks", "vs", "lfs", "convp", "convs",
                            "cp", "np", "mp", "cs", "ns", "ms")}
    for l in range(depth):
        (sh1p, sc1p, gt1p, sh2p, sc2p, gt2p), (sh1s, sc1s, gt1s, sh2s, sc2s, gt2s) = _mod_parts(mod_all[l], bp, ls)
        w_main = jnp.concatenate([w_in[l][:, :n_main], w_in[l][:, a0:f0]], axis=1).astype(BF16)
        gate_w = jnp.concatenate([w_in[l][:, g0:a0], w_in[l][:, f0:]], axis=1)
        gate_w = jnp.pad(gate_w, ((0, 0), (0, GATE_COLS - gate_w.shape[1]))).astype(BF16)
        gate_bias = jnp.concatenate([mlstm_i_bias[l], mlstm_f_bias[l], fox_f_bias[l]])
        bias_row = jnp.pad(gate_bias, (0, GATE_COLS - gate_bias.shape[0]))[None, :]
        bias_col = gate_bias[:2 * mh, None]
        w_out_b = w_out[l].astype(BF16)
        w_router_t = w_router[l].T
        wsg, wsu, wsd = w_sh_gate[l].astype(BF16), w_sh_up[l].astype(BF16), w_sh_down[l].astype(BF16)
        gpm, gqm = g_pre_mix[l][None, :], g_post_mix[l][None, :]
        gpf, gqf = g_pre_ffn[l][None, :], g_post_ffn[l][None, :]
        dw, db = conv_dw[l], conv_db[l][None, :]
        lng, lnb = conv_ln_g[l][None, :], conv_ln_b[l][None, :]

        groups = []
        for (x, sc1, sh1, tm) in ((xp, sc1p, sh1p, 512), (xs, sc1s, sh1s, tm_s)):
            u = _inproj(x, gpm, sc1, sh1, w_main, tm, 1024)
            gates = _inproj(x, gpm, sc1, sh1, gate_w, tm, GATE_COLS)
            groups.append((u, gates))
        (u_p, gates_p), (u_s, gates_s) = groups

        yc_p, conv_p = _conv(u_p, zero_conv, dw, db, lng, lnb, bp, lp, min(512, lp))
        yc_s, conv_s = _conv(u_s, state_conv[l], dw, db, lng, lnb, bs, ls, ls)

        def gates_t(gates, nb, seq):
            c = MLSTM_CHUNK if seq % MLSTM_CHUNK == 0 else seq
            return gates[:, :2 * mh].reshape(nb, seq // c, c, 2 * mh).transpose(0, 1, 3, 2)

        mcol0 = (2 * cc) // mw
        ym_p, c_p, n_p, m_p = _mlstm(u_p, gates_p, gates_t(gates_p, bp, lp), bias_row, bias_col,
                                     zero_c, zero_n, zero_m, bp, lp, mcol0)
        ym_s, c_s, n_s, m_s = _mlstm(u_s, gates_s, gates_t(gates_s, bs, ls), bias_row, bias_col,
                                     state_mlstm_C[l], state_mlstm_n[l], state_mlstm_m[l], bs, ls, mcol0)

        fb = fox_f_bias[l]
        lf_p = _log_sigmoid(gates_p[:, 2 * mh:2 * mh + ah] + fb).reshape(bp, lp, ah)
        lf_s = _log_sigmoid(gates_s[:, 2 * mh:2 * mh + ah] + fb).reshape(bs, ls, ah)
        fcum = jnp.cumsum(lf_p, axis=1).transpose(0, 2, 1)
        qcol = n_main // HEAD_DIM
        ya_p = _fox_prompt(u_p, fcum[..., None], fcum[:, :, None, :], bp, lp, ah,
                           qcol, qcol + ah, qcol + 2 * ah)
        k_s = u_s[:, n_main + aw:n_main + 2 * aw]
        v_s = u_s[:, n_main + 2 * aw:n_main + 3 * aw]
        past = _past_bias(cache_lft, page_table, l)
        past = past.transpose(0, 2, 1).reshape(bs, n_pages, 1, page * ah)
        ya_s = _fox_sample(u_s[:, n_main:n_main + aw].reshape(ts * ah, HEAD_DIM),
                           k_s.reshape(ts * ah, HEAD_DIM), v_s.reshape(ts * ah, HEAD_DIM),
                           lf_s.reshape(ts * ah, 1), lf_s.reshape(bs, 1, ls * ah),
                           cache_k2, cache_v2, past, page_table, l, ah).reshape(ts, aw)

        x1_p, h2_p, h2slab_p, sco_p = _outproj(yc_p, ym_p, ya_p, xp, w_out_b, gqm, gt1p, gpf, sc2p, sh2p,
                                               w_router_t, 256)
        x1_s, h2_s, h2slab_s, sco_s = _outproj(yc_s, ym_s, ya_s, xs, w_out_b, gqm, gt1s, gpf, sc2s, sh2s,
                                               w_router_t, tm_s)

        idx_t, gw_t, rank_t, counts = _route(jnp.concatenate([sco_p, sco_s], axis=1), router_bias[l])
        row_tok, pos, block_e, n_active = _dispatch(idx_t, rank_t, counts)
        h2_all = jnp.concatenate([h2slab_p, h2slab_s], axis=0)
        ysort = _experts(h2_all, row_tok, block_e, n_active, w_exp_gate, w_exp_up, w_exp_down, l)
        gw = gw_t.T
        xp = _combine(ysort, pos[:tp], x1_p, h2_p, gw[:tp], wsg, wsu, wsd, gqf, gt2p, 128)
        xs = _combine(ysort, pos[tp:], x1_s, h2_s, gw[tp:], wsg, wsu, wsd, gqf, gt2s, min(128, ts))

        outs["kp"].append(u_p[:, n_main + aw:n_main + 2 * aw].reshape(bp, lp, ah, HEAD_DIM))
        outs["vp"].append(u_p[:, n_main + 2 * aw:n_main + 3 * aw].reshape(bp, lp, ah, HEAD_DIM))
        outs["lfp"].append(lf_p)
        outs["ks"].append(k_s.reshape(bs, ls, ah, HEAD_DIM))
        outs["vs"].append(v_s.reshape(bs, ls, ah, HEAD_DIM))
        outs["lfs"].append(lf_s)
        outs["convp"].append(conv_p)
        outs["convs"].append(conv_s)
        outs["cp"].append(c_p)
        outs["np"].append(n_p)
        outs["mp"].append(m_p.reshape(bp, mh))
        outs["cs"].append(c_s)
        outs["ns"].append(n_s)
        outs["ms"].append(m_s.reshape(bs, mh))

    st = {k: jnp.stack(v) for k, v in outs.items()}
    return (xp.reshape(bp, lp, d), xs.reshape(bs, ls, d),
            st["kp"], st["vp"], st["lfp"], st["ks"], st["vs"], st["lfs"],
            st["convp"], st["convs"], st["cp"], st["np"], st["mp"], st["cs"], st["ns"], st["ms"])
```

```python
import functools

import jax
import jax.numpy as jnp
from jax import lax
from jax.experimental import pallas as pl
from jax.experimental.pallas import tpu as pltpu

F32 = jnp.float32
BF16 = jnp.bfloat16
I32 = jnp.int32
HI = lax.Precision.HIGHEST

HEAD_DIM = 128
CONV_W = 31
MLSTM_CHUNK = 64
N_EXPERT_GROUPS = 8
TOPK_GROUPS = 4
TOP_K = 8
ROUTED_SCALE = 2.5
NORM_EPS = 1e-6

LANES = 128
GATE_COLS = 128
EXPERT_BLOCK = 256
VMEM_LIMIT = 56 * 1024 * 1024
WEIGHT_DMA_PRIORITY = 1
ROW_AHEAD = 2
ROW_SLOTS = ROW_AHEAD + 1


def _params(sem):
    return pltpu.CompilerParams(dimension_semantics=sem, vmem_limit_bytes=VMEM_LIMIT)


def _log_sigmoid(x):
    return jnp.minimum(x, 0.0) - jnp.log1p(jnp.exp(-jnp.abs(x)))


def _silu(x):
    return x * jax.nn.sigmoid(x)


def _rms(x):
    return x * lax.rsqrt(jnp.mean(x * x, axis=-1, keepdims=True) + NORM_EPS)


def _dot(a, b):
    return jnp.dot(a, b, preferred_element_type=F32)


def _dot_nt(a, b):
    return lax.dot_general(a, b, (((1,), (1,)), ((), ())), preferred_element_type=F32)


def _dot_tn(a, b):
    return lax.dot_general(a, b, (((0,), (0,)), ((), ())), preferred_element_type=F32)


def _dot_hi(a, b):
    return jnp.dot(a, b, preferred_element_type=F32, precision=HI)


def _split3(a):
    hi = a.astype(BF16)
    rest = a - hi.astype(F32)
    mid = rest.astype(BF16)
    lo = (rest - mid.astype(F32)).astype(BF16)
    return hi, mid, lo


def _dot_ones(a, ones):
    ob = ones.astype(BF16)
    hi, mid, lo = _split3(a)
    return _dot(hi, ob) + _dot(mid, ob) + _dot(lo, ob)


def _ones_dot(ones, a):
    ob = ones.astype(BF16)
    hi, mid, lo = _split3(a)
    return _dot(ob, hi) + _dot(ob, mid) + _dot(ob, lo)


def _lane_tile(t, cap):
    best = LANES
    for k in range(1, cap // LANES + 1):
        if t % (k * LANES) == 0:
            best = k * LANES
    return best


def _store_slabs(ref, x):
    rows, d = x.shape
    nslab = d // LANES
    for j in range(nslab):
        ref[pl.ds(j, rows, stride=nslab), :] = x[:, j * LANES:(j + 1) * LANES]


def _load_slabs(ref, rows, nslab, lead=()):
    return jnp.concatenate([ref[lead + (pl.ds(j, rows, stride=nslab), slice(None))] for j in range(nslab)],
                           axis=1)


def _ada_kernel(c_ref, w_ref, b_ref, o_ref):
    s = _silu(c_ref[...]).astype(BF16)
    o_ref[0] = _dot(s, w_ref[0].astype(BF16)) + b_ref[0]


def _ada(c_all, w_ada, b_ada):
    depth, d, n = w_ada.shape
    r = c_all.shape[0]
    tn = 1024
    return pl.pallas_call(
        _ada_kernel,
        grid=(depth, n // tn),
        in_specs=[pl.BlockSpec((r, d), lambda l, j: (0, 0)),
                  pl.BlockSpec((1, d, tn), lambda l, j: (l, 0, j)),
                  pl.BlockSpec((1, 1, tn), lambda l, j: (l, 0, j))],
        out_specs=pl.BlockSpec((1, r, tn), lambda l, j: (l, 0, j)),
        out_shape=jax.ShapeDtypeStruct((depth, r, n), F32),
        compiler_params=_params(("arbitrary", "arbitrary")),
        name="ada",
    )(c_all, w_ada, b_ada.reshape(depth, 1, n))


def _mod_spec(mod, t, tm, row_axis):
    ngrp, r, d = mod.shape
    if r == 1:
        tiles_per_grp = (t // ngrp) // tm
        return pl.BlockSpec((1, 1, d), lambda *ix: (ix[row_axis] // tiles_per_grp, 0, 0))
    return pl.BlockSpec((1, tm, d), lambda *ix: (0, ix[row_axis], 0))


def _inproj_kernel(x_ref, g_ref, sc_ref, sh_ref, w_ref, o_ref):
    h = (_rms(x_ref[...]) * g_ref[...]) * (1.0 + sc_ref[0]) + sh_ref[0]
    o_ref[...] = _dot(h.astype(BF16), w_ref[...])


def _inproj(x, g, sc, sh, w, tm, tn):
    t, d = x.shape
    n = w.shape[1]
    mod_spec = _mod_spec(sc, t, tm, 1)
    return pl.pallas_call(
        _inproj_kernel,
        grid=(n // tn, t // tm),
        in_specs=[pl.BlockSpec((tm, d), lambda j, i: (i, 0)),
                  pl.BlockSpec((1, d), lambda j, i: (0, 0)),
                  mod_spec, mod_spec,
                  pl.BlockSpec((d, tn), lambda j, i: (0, j))],
        out_specs=pl.BlockSpec((tm, tn), lambda j, i: (i, j)),
        out_shape=jax.ShapeDtypeStruct((t, n), F32),
        compiler_params=_params(("arbitrary", "arbitrary")),
        name="inproj",
    )(x, g, sc, sh, w)


def _conv_kernel(a_ref, g_ref, buf_ref, dw_ref, db_ref, lg_ref, lb_ref, y_ref, st_ref, full_ref,
                 *, lc, nchunks):
    halo = CONV_W + 1
    j = pl.program_id(1)

    @pl.when(j == 0)
    def _():
        full_ref[0:2, :] = jnp.zeros((2, full_ref.shape[1]), F32)
        full_ref[2:halo, :] = buf_ref[0]

    u = a_ref[...] * jax.nn.sigmoid(g_ref[...])
    full_ref[halo:halo + lc, :] = u
    acc = jnp.zeros_like(u)
    for w in range(CONV_W):
        acc = acc + full_ref[2 + w:2 + w + lc, :] * dw_ref[w:w + 1, :]
    acc = acc + db_ref[...]
    mu = jnp.mean(acc, axis=-1, keepdims=True)
    cen = acc - mu
    var = jnp.mean(cen * cen, axis=-1, keepdims=True)
    y = cen * lax.rsqrt(var + NORM_EPS) * lg_ref[...] + lb_ref[...]
    y_ref[...] = _silu(y)

    @pl.when(j == nchunks - 1)
    def _():
        st_ref[0] = full_ref[2 + lc:halo + lc, :]

    if nchunks > 1:
        full_ref[0:halo, :] = full_ref[lc:lc + halo, :]


def _conv(u, buf, dw, db, lg, lb, nb, seq, lc):
    c = buf.shape[2]
    nchunks = seq // lc
    vec = pl.BlockSpec((1, c), lambda b, j: (0, 0))
    return pl.pallas_call(
        functools.partial(_conv_kernel, lc=lc, nchunks=nchunks),
        grid=(nb, nchunks),
        in_specs=[pl.BlockSpec((lc, c), lambda b, j: (b * nchunks + j, 0)),
                  pl.BlockSpec((lc, c), lambda b, j: (b * nchunks + j, 1)),
                  pl.BlockSpec((1, CONV_W - 1, c), lambda b, j: (b, 0, 0)),
                  pl.BlockSpec((CONV_W, c), lambda b, j: (0, 0)),
                  vec, vec, vec],
        out_specs=[pl.BlockSpec((lc, c), lambda b, j: (b * nchunks + j, 0)),
                   pl.BlockSpec((1, CONV_W - 1, c), lambda b, j: (b, 0, 0))],
        out_shape=[jax.ShapeDtypeStruct((nb * seq, c), F32),
                   jax.ShapeDtypeStruct((nb, CONV_W - 1, c), F32)],
        scratch_shapes=[pltpu.VMEM((CONV_W + 1 + lc, c), F32)],
        compiler_params=_params(("arbitrary", "arbitrary")),
        name="conv",
    )(u, u, buf, dw, db, lg, lb)


def _mlstm_kernel(q_ref, k_ref, v_ref, o_ref, gc_ref, gr_ref, bc_ref, br_ref, c0_ref, n0_ref, m0_ref,
                  y_ref, cout_ref, nout_ref, mout_ref, c_s, n_s, m_s, *, c, nc, nh, nbs):
    j = pl.program_id(1)

    @pl.when(j == 0)
    def _():
        c_s[...] = c0_ref[...]
        n_s[...] = n0_ref[...]
        for b in range(nbs):
            for h in range(nh):
                m_s[b, h] = jnp.broadcast_to(m0_ref[b][:, h:h + 1], m_s.shape[2:])

    for b in range(nbs):
        _mlstm_chunk(b, q_ref, k_ref, v_ref, o_ref, gc_ref, gr_ref, bc_ref, br_ref, y_ref, c_s, n_s, m_s,
                     c=c, nh=nh)

    @pl.when(j == nc - 1)
    def _():
        cout_ref[...] = c_s[...]
        nout_ref[...] = n_s[...]
        lane = lax.broadcasted_iota(I32, (1, nh), 1)
        for b in range(nbs):
            m_out = jnp.zeros((1, nh), F32)
            for h in range(nh):
                m_out = jnp.where(lane == h, m_s[b, h][0:1, 0:1], m_out)
            mout_ref[b] = m_out


def _mlstm_chunk(b, q_ref, k_ref, v_ref, o_ref, gc_ref, gr_ref, bc_ref, br_ref, y_ref, c_s, n_s, m_s, *, c, nh):
    scale = HEAD_DIM ** -0.5
    mm = BF16 if c % 16 == 0 else F32
    gcol = gc_ref[b] + bc_ref[...]
    grow = gr_ref[b, 0] + br_ref[...]
    lf_col = _log_sigmoid(gcol)
    lf_row = _log_sigmoid(grow)
    row = lax.broadcasted_iota(I32, (c, c), 0)
    col = lax.broadcasted_iota(I32, (c, c), 1)
    causal = col <= row
    if mm == BF16:
        b_col_all = _ones_dot(causal, lf_col)
        b_row_all = _dot_ones(lf_row, row <= col)
    else:
        b_col_all = _dot_hi(causal.astype(F32), lf_col)
        b_row_all = _dot_hi(lf_row, (row <= col).astype(F32))

    for h in range(nh):
        sl = slice(h * HEAD_DIM, (h + 1) * HEAD_DIM)
        q = q_ref[b, :, sl]
        k = k_ref[b, :, sl] * scale
        v = v_ref[b, :, sl]
        b_c = b_col_all[:, nh + h:nh + h + 1]
        b_r = b_row_all[nh + h:nh + h + 1, :]
        ig_r = grow[h:h + 1, :]
        ig_c = gcol[:, h:h + 1]
        m_prev = m_s[b, h][0:1, 0:1]
        dmat = jnp.where(causal, b_c - b_r + ig_r, -jnp.inf)
        inter = b_c + m_prev
        m_t = jnp.maximum(inter, jnp.max(dmat, axis=1, keepdims=True))
        dw = jnp.exp(dmat - m_t)
        iw = jnp.exp(inter - m_t)
        qm, km, vm = q.astype(mm), k.astype(mm), v.astype(mm)
        s = _dot_nt(qm, km) * dw
        c_h = c_s[b, h]
        n_h = n_s[b, h:h + 1, :]
        num = _dot(s.astype(mm), vm) + iw * _dot(qm, c_h.astype(mm))
        den = jnp.sum(s, axis=1, keepdims=True) + iw * jnp.sum(q * n_h, axis=1, keepdims=True)
        hid = num / jnp.maximum(jnp.abs(den), jnp.exp(-m_t))
        y_ref[b, :, sl] = jax.nn.sigmoid(o_ref[b, :, sl]) * hid
        m_last = m_t[c - 1:c, :]
        w_end = jnp.exp(b_c[c - 1:c, :] - b_c + ig_c - m_last)
        f_end = iw[c - 1:c, :]
        kw = k * w_end
        c_s[b, h] = f_end * c_h + _dot_tn(kw.astype(mm), vm)
        n_s[b, h:h + 1, :] = f_end * n_h + jnp.sum(kw, axis=0, keepdims=True)
        m_s[b, h] = jnp.broadcast_to(m_last, m_s.shape[2:])


def _mlstm(u, gates, gates_t, bias_row, bias_col, c0, n0, m0, nb, seq, col0):
    nh = c0.shape[1]
    w = nh * HEAD_DIM
    c = MLSTM_CHUNK if seq % MLSTM_CHUNK == 0 else seq
    nc = seq // c
    nbs = 4 if nb % 4 == 0 else 1
    u3 = u.reshape(nb, seq, u.shape[1])

    def ublk(off):
        return pl.BlockSpec((nbs, c, w), lambda g, j: (g, j, col0 + off))

    y, c_out, n_out, m_out = pl.pallas_call(
        functools.partial(_mlstm_kernel, c=c, nc=nc, nh=nh, nbs=nbs),
        grid=(nb // nbs, nc),
        in_specs=[ublk(0), ublk(1), ublk(2), ublk(3),
                  pl.BlockSpec((nbs, c, GATE_COLS), lambda g, j: (g, j, 0)),
                  pl.BlockSpec((nbs, 1, 2 * nh, c), lambda g, j: (g, j, 0, 0)),
                  pl.BlockSpec((1, GATE_COLS), lambda g, j: (0, 0)),
                  pl.BlockSpec((2 * nh, 1), lambda g, j: (0, 0)),
                  pl.BlockSpec((nbs, nh, HEAD_DIM, HEAD_DIM), lambda g, j: (g, 0, 0, 0)),
                  pl.BlockSpec((nbs, nh, HEAD_DIM), lambda g, j: (g, 0, 0)),
                  pl.BlockSpec((nbs, 1, nh), lambda g, j: (g, 0, 0))],
        out_specs=[pl.BlockSpec((nbs, c, w), lambda g, j: (g, j, 0)),
                   pl.BlockSpec((nbs, nh, HEAD_DIM, HEAD_DIM), lambda g, j: (g, 0, 0, 0)),
                   pl.BlockSpec((nbs, nh, HEAD_DIM), lambda g, j: (g, 0, 0)),
                   pl.BlockSpec((nbs, 1, nh), lambda g, j: (g, 0, 0))],
        out_shape=[jax.ShapeDtypeStruct((nb, seq, w), F32),
                   jax.ShapeDtypeStruct((nb, nh, HEAD_DIM, HEAD_DIM), F32),
                   jax.ShapeDtypeStruct((nb, nh, HEAD_DIM), F32),
                   jax.ShapeDtypeStruct((nb, 1, nh), F32)],
        scratch_shapes=[pltpu.VMEM((nbs, nh, HEAD_DIM, HEAD_DIM), F32),
                        pltpu.VMEM((nbs, nh, HEAD_DIM), F32),
                        pltpu.VMEM((nbs, nh, 8, LANES), F32)],
        compiler_params=_params(("arbitrary", "arbitrary")),
        name="mlstm",
    )(u3, u3, u3, u3, gates.reshape(nb, seq, GATE_COLS), gates_t, bias_row, bias_col, c0, n0,
      m0.reshape(nb, 1, nh))
    return y.reshape(nb * seq, w), c_out, n_out, m_out


def _fox_prompt_kernel(q_ref, k_ref, v_ref, fc_ref, fr_ref, o_ref, *, seq, tq):
    scale = HEAD_DIM ** -0.5
    kb = k_ref[...].astype(BF16)
    vb = v_ref[...].astype(BF16)
    f_row = fr_ref[0, 0]
    for qi in range(seq // tq):
        rows = slice(qi * tq, (qi + 1) * tq)
        kend = (qi + 1) * tq
        q = (q_ref[rows, :] * scale).astype(BF16)
        s = _dot_nt(q, kb[:kend]) + fc_ref[0, 0][rows, :] - f_row[:, :kend]
        qpos = lax.broadcasted_iota(I32, (tq, kend), 0) + qi * tq
        kpos = lax.broadcasted_iota(I32, (tq, kend), 1)
        s = jnp.where(kpos <= qpos, s, -jnp.inf)
        p = jnp.exp(s - jnp.max(s, axis=1, keepdims=True))
        denom = jnp.sum(p, axis=1, keepdims=True)
        o_ref[rows, :] = _dot(p.astype(BF16), vb[:kend]) / denom


def _fox_prompt(u, f_col, f_row, nb, seq, nh, qcol, kcol, vcol):
    tq = min(256, seq)
    return pl.pallas_call(
        functools.partial(_fox_prompt_kernel, seq=seq, tq=tq),
        grid=(nb, nh),
        in_specs=[pl.BlockSpec((seq, HEAD_DIM), lambda b, h: (b, qcol + h)),
                  pl.BlockSpec((seq, HEAD_DIM), lambda b, h: (b, kcol + h)),
                  pl.BlockSpec((seq, HEAD_DIM), lambda b, h: (b, vcol + h)),
                  pl.BlockSpec((1, 1, seq, 1), lambda b, h: (b, h, 0, 0)),
                  pl.BlockSpec((1, 1, 1, seq), lambda b, h: (b, h, 0, 0))],
        out_specs=pl.BlockSpec((seq, HEAD_DIM), lambda b, h: (b, h)),
        out_shape=jax.ShapeDtypeStruct((nb * seq, nh * HEAD_DIM), F32),
        compiler_params=_params(("arbitrary", "arbitrary")),
        name="fox_prompt",
    )(u, u, u, f_col, f_row)


def _pages_per_step(n_pages):
    return 8 if n_pages % 8 == 0 else 1


def _past_bias_kernel(pt_ref, *rest, npg):
    lf_refs = rest[:npg]
    o_ref, carry_s = rest[npg:]
    page = lf_refs[0].shape[3]

    @pl.when(pl.program_id(1) == 0)
    def _():
        carry_s[...] = jnp.zeros(carry_s.shape, F32)

    later = lax.broadcasted_iota(I32, (page, page), 0) > lax.broadcasted_iota(I32, (page, page), 1)
    carry = carry_s[...]
    for r in reversed(range(npg)):
        lft = lf_refs[r][0, 0]
        o_ref[0, :, r * page:(r + 1) * page] = carry + _dot_ones(lft, later)
        carry = carry + jnp.sum(lft, axis=1, keepdims=True)
    carry_s[...] = carry


def _past_bias(cache_lft, page_table, layer):
    nb, n_pages = page_table.shape
    nh, page = cache_lft.shape[2], cache_lft.shape[3]
    npg = 64 if n_pages % 64 == 0 else _pages_per_step(n_pages)
    ngrp = n_pages // npg
    specs = [pl.BlockSpec((1, 1, nh, page),
                          lambda b, g, pt, r=r: (layer, pt[b, (ngrp - 1 - g) * npg + r], 0, 0))
             for r in range(npg)]
    grid_spec = pltpu.PrefetchScalarGridSpec(
        num_scalar_prefetch=1,
        grid=(nb, ngrp),
        in_specs=specs,
        out_specs=pl.BlockSpec((1, nh, npg * page), lambda b, g, pt: (b, 0, ngrp - 1 - g)),
        scratch_shapes=[pltpu.VMEM((nh, 1), F32)],
    )
    return pl.pallas_call(
        functools.partial(_past_bias_kernel, npg=npg),
        grid_spec=grid_spec,
        out_shape=jax.ShapeDtypeStruct((nb, nh, n_pages * page), F32),
        compiler_params=_params(("arbitrary", "arbitrary")),
        name="fox_past_bias",
    )(page_table, *([cache_lft] * npg))


def _fox_sample_kernel(pt_ref, q_ref, kn_ref, vn_ref, lfc_ref, lfr_ref, *rest, npg, ngrp, nh):
    k_refs = rest[0:npg]
    v_refs = rest[npg:2 * npg]
    b_refs = rest[2 * npg:3 * npg]
    o_ref, m_s, l_s, acc_s = rest[3 * npg:]
    g = pl.program_id(1)
    scale = HEAD_DIM ** -0.5
    nr = q_ref.shape[0]
    ncol = k_refs[0].shape[2]
    hshift = nh.bit_length() - 1
    qb = (q_ref[...] * scale).astype(BF16)

    ri = lax.broadcasted_iota(I32, (nr, nr), 0)
    ci = lax.broadcasted_iota(I32, (nr, nr), 1)
    same_head = (ri & (nh - 1)) == (ci & (nh - 1))
    col_le_row = jnp.logical_and(same_head, (ci >> hshift) <= (ri >> hshift))
    row_le_col = jnp.logical_and(same_head, (ri >> hshift) <= (ci >> hshift))
    f_col = jnp.sum(jnp.where(col_le_row, lfr_ref[0], 0.0), axis=1, keepdims=True)

    @pl.when(g == 0)
    def _():
        f_row = jnp.sum(jnp.where(row_le_col, lfc_ref[...], 0.0), axis=0, keepdims=True)
        s = _dot_nt(qb, kn_ref[...].astype(BF16)) + f_col - f_row
        s = jnp.where(col_le_row, s, -jnp.inf)
        m = jnp.max(s, axis=1, keepdims=True)
        p = jnp.exp(s - m)
        m_s[...] = m
        l_s[...] = jnp.sum(p, axis=1, keepdims=True)
        acc_s[...] = _dot(p.astype(BF16), vn_ref[...].astype(BF16))

    rr = lax.broadcasted_iota(I32, (nr, ncol), 0)
    cc = lax.broadcasted_iota(I32, (nr, ncol), 1)
    base = f_col + jnp.where((rr & (nh - 1)) == (cc & (nh - 1)), 0.0, -jnp.inf)
    s = jnp.concatenate([_dot_nt(qb, k_refs[r][0, 0].astype(BF16)) + (base + b_refs[r][0, 0])
                         for r in range(npg)], axis=1)
    m_old = m_s[...]
    m_new = jnp.maximum(m_old, jnp.max(s, axis=1, keepdims=True))
    alpha = jnp.exp(m_old - m_new)
    p = jnp.exp(s - m_new)
    l_s[...] = alpha * l_s[...] + jnp.sum(p, axis=1, keepdims=True)
    pv = _dot(p[:, 0:ncol].astype(BF16), v_refs[0][0, 0].astype(BF16))
    for r in range(1, npg):
        pv = pv + _dot(p[:, r * ncol:(r + 1) * ncol].astype(BF16), v_refs[r][0, 0].astype(BF16))
    acc_s[...] = alpha * acc_s[...] + pv
    m_s[...] = m_new

    @pl.when(g == ngrp - 1)
    def _():
        o_ref[...] = acc_s[...] / l_s[...]


def _fox_sample(q2, kn2, vn2, lf_col, lf_row, cache_k2, cache_v2, bias, page_table, layer, nh):
    nb, n_pages = page_table.shape
    nr = q2.shape[0] // nb
    ncol = cache_k2.shape[2]
    npg = _pages_per_step(n_pages)
    ngrp = n_pages // npg
    assert nh & (nh - 1) == 0

    kv_specs = [pl.BlockSpec((1, 1, ncol, HEAD_DIM), lambda b, g, pt, r=r: (layer, pt[b, g * npg + r], 0, 0))
                for r in range(npg)]
    b_specs = [pl.BlockSpec((1, 1, 1, ncol), lambda b, g, pt, r=r: (b, g * npg + r, 0, 0)) for r in range(npg)]
    rows = pl.BlockSpec((nr, HEAD_DIM), lambda b, g, pt: (b, 0))
    grid_spec = pltpu.PrefetchScalarGridSpec(
        num_scalar_prefetch=1,
        grid=(nb, ngrp),
        in_specs=[rows, rows, rows,
                  pl.BlockSpec((nr, 1), lambda b, g, pt: (b, 0)),
                  pl.BlockSpec((1, 1, nr), lambda b, g, pt: (b, 0, 0))] + kv_specs + kv_specs + b_specs,
        out_specs=rows,
        scratch_shapes=[pltpu.VMEM((nr, 1), F32), pltpu.VMEM((nr, 1), F32), pltpu.VMEM((nr, HEAD_DIM), F32)],
    )
    return pl.pallas_call(
        functools.partial(_fox_sample_kernel, npg=npg, ngrp=ngrp, nh=nh),
        grid_spec=grid_spec,
        out_shape=jax.ShapeDtypeStruct(q2.shape, F32),
        compiler_params=_params(("arbitrary", "arbitrary")),
        name="fox_sample",
    )(page_table, q2, kn2, vn2, lf_col, lf_row, *([cache_k2] * npg), *([cache_v2] * npg), *([bias] * npg))


def _outproj_kernel(yc_ref, ym_ref, ya_ref, x_ref, wc_ref, wm_ref, wa_ref, gpost_ref, gt_ref, gpre_ref,
                    sc_ref, sh_ref, wrt_ref, x1_ref, h2_ref, h2s_ref, sco_ref):
    mix = (_dot(yc_ref[...].astype(BF16), wc_ref[...]) + _dot(ym_ref[...].astype(BF16), wm_ref[...])
           + _dot(ya_ref[...].astype(BF16), wa_ref[...]))
    x1 = x_ref[...] + gt_ref[0] * (_rms(mix) * gpost_ref[...])
    x1_ref[...] = x1
    h2 = (_rms(x1) * gpre_ref[...]) * (1.0 + sc_ref[0]) + sh_ref[0]
    h2_ref[...] = h2
    _store_slabs(h2s_ref, h2)
    w_hi, w_lo, _ = _split3(wrt_ref[...])
    h_hi, h_lo, _ = _split3(h2)
    logits_t = _dot_nt(w_hi, h_hi) + (_dot_nt(w_hi, h_lo) + _dot_nt(w_lo, h_hi))
    sco_ref[...] = jax.nn.sigmoid(logits_t)


def _outproj(yc, ym, ya, x, w_out, gpost, gt, gpre, sc, sh, w_router_t, tm):
    t, d = x.shape
    cw, mw, aw = yc.shape[1], ym.shape[1], ya.shape[1]
    ne = w_router_t.shape[0]
    mod_spec = _mod_spec(sc, t, tm, 0)
    vec = pl.BlockSpec((1, d), lambda i: (0, 0))
    return pl.pallas_call(
        _outproj_kernel,
        grid=(t // tm,),
        in_specs=[pl.BlockSpec((tm, cw), lambda i: (i, 0)),
                  pl.BlockSpec((tm, mw), lambda i: (i, 0)),
                  pl.BlockSpec((tm, aw), lambda i: (i, 0)),
                  pl.BlockSpec((tm, d), lambda i: (i, 0)),
                  pl.BlockSpec((cw, d), lambda i: (0, 0)),
                  pl.BlockSpec((mw, d), lambda i: (cw // mw, 0)),
                  pl.BlockSpec((aw, d), lambda i: ((cw + mw) // aw, 0)),
                  vec, mod_spec, vec, mod_spec, mod_spec,
                  pl.BlockSpec((ne, d), lambda i: (0, 0))],
        out_specs=[pl.BlockSpec((tm, d), lambda i: (i, 0)),
                   pl.BlockSpec((tm, d), lambda i: (i, 0)),
                   pl.BlockSpec((tm * (d // LANES), LANES), lambda i: (i, 0)),
                   pl.BlockSpec((ne, tm), lambda i: (0, i))],
        out_shape=[jax.ShapeDtypeStruct((t, d), F32),
                   jax.ShapeDtypeStruct((t, d), F32),
                   jax.ShapeDtypeStruct((t * (d // LANES), LANES), F32),
                   jax.ShapeDtypeStruct((ne, t), F32)],
        compiler_params=_params(("arbitrary",)),
        name="outproj",
    )(yc, ym, ya, x, w_out, w_out, w_out, gpost, gt, gpre, sc, sh, w_router_t)


def _route_kernel(s_ref, b_ref, idx_ref, w_ref, rank_ref, cnt_ref, carry_s, *, ne, ng, tt, ntiles):
    i = pl.program_id(0)
    gs = ne // ng

    @pl.when(i == 0)
    def _():
        carry_s[...] = jnp.zeros(carry_s.shape, F32)

    def first_max(x, ids, n):
        m = jnp.max(x, axis=0, keepdims=True)
        return m, jnp.min(jnp.where(x == m, ids, float(n)), axis=0, keepdims=True)

    s = s_ref[...]
    sel = s + b_ref[...]
    sub = lax.broadcasted_iota(I32, (gs, tt), 0).astype(F32)
    grow = lax.broadcasted_iota(I32, (ng, tt), 0).astype(F32)
    gscore = jnp.zeros((ng, tt), F32)
    for g in range(ng):
        blk = sel[g * gs:(g + 1) * gs, :]
        m1, i1 = first_max(blk, sub, gs)
        m2 = jnp.max(jnp.where(sub == i1, -jnp.inf, blk), axis=0, keepdims=True)
        gscore = jnp.where(grow == g, m1 + m2, gscore)
    gsel = jnp.zeros((ng, tt), F32)
    for _ in range(TOPK_GROUPS):
        _, ig = first_max(gscore, grow, ng)
        hit = grow == ig
        gsel = jnp.where(hit, 1.0, gsel)
        gscore = jnp.where(hit, -jnp.inf, gscore)
    cand = jnp.concatenate([jnp.where(gsel[g:g + 1, :] > 0.0, sel[g * gs:(g + 1) * gs, :], -jnp.inf)
                            for g in range(ng)], axis=0)

    erow = lax.broadcasted_iota(I32, (ne, tt), 0).astype(F32)
    member = jnp.zeros((ne, tt), F32)
    picks, weights = [], []
    for _ in range(TOP_K):
        _, ie = first_max(cand, erow, ne)
        hit = erow == ie
        picks.append(ie)
        weights.append(jnp.sum(jnp.where(hit, s, 0.0), axis=0, keepdims=True))
        member = jnp.where(hit, 1.0, member)
        cand = jnp.where(hit, -jnp.inf, cand)
    wsum = weights[0]
    for k in range(1, TOP_K):
        wsum = wsum + weights[k]

    upto = lax.broadcasted_iota(I32, (tt, tt), 0) <= lax.broadcasted_iota(I32, (tt, tt), 1)
    incl = _dot(member.astype(BF16), upto.astype(BF16)) + carry_s[...]
    rank = incl - member
    carry_s[...] = incl[:, tt - 1:tt]

    krow = lax.broadcasted_iota(I32, (TOP_K, tt), 0)
    idx_out = jnp.zeros((TOP_K, tt), F32)
    w_out = jnp.zeros((TOP_K, tt), F32)
    rank_out = jnp.zeros((TOP_K, tt), F32)
    for k in range(TOP_K):
        idx_out = jnp.where(krow == k, picks[k], idx_out)
        w_out = jnp.where(krow == k, weights[k] / wsum * ROUTED_SCALE, w_out)
        rk = jnp.sum(jnp.where(erow == picks[k], rank, 0.0), axis=0, keepdims=True)
        rank_out = jnp.where(krow == k, rk, rank_out)
    idx_ref[...] = idx_out.astype(I32)
    w_ref[...] = w_out
    rank_ref[...] = rank_out

    @pl.when(i == ntiles - 1)
    def _():
        cnt_ref[...] = carry_s[...]


def _route(scores_t, router_bias):
    ne, t = scores_t.shape
    tt = _lane_tile(t, 768)
    ntiles = t // tt
    tok = pl.BlockSpec((TOP_K, tt), lambda i: (0, i))
    return pl.pallas_call(
        functools.partial(_route_kernel, ne=ne, ng=N_EXPERT_GROUPS, tt=tt, ntiles=ntiles),
        grid=(ntiles,),
        in_specs=[pl.BlockSpec((ne, tt), lambda i: (0, i)),
                  pl.BlockSpec((ne, 1), lambda i: (0, 0))],
        out_specs=[tok, tok, tok, pl.BlockSpec((ne, 1), lambda i: (0, 0))],
        out_shape=[jax.ShapeDtypeStruct((TOP_K, t), I32),
                   jax.ShapeDtypeStruct((TOP_K, t), F32),
                   jax.ShapeDtypeStruct((TOP_K, t), F32),
                   jax.ShapeDtypeStruct((ne, 1), F32)],
        scratch_shapes=[pltpu.VMEM((ne, 1), F32)],
        compiler_params=_params(("arbitrary",)),
        name="moe_route",
    )(scores_t, router_bias.astype(F32)[:, None])


def _dispatch(idx_t, rank_t, counts):
    topk, t = idx_t.shape
    ne = counts.shape[0]
    a = t * topk
    counts = counts.reshape(ne).astype(I32)
    padded = (counts + EXPERT_BLOCK - 1) // EXPERT_BLOCK * EXPERT_BLOCK
    pends = jnp.cumsum(padded)
    pstarts = pends - padded
    start_of = jnp.sum(jnp.where(idx_t[:, :, None] == jnp.arange(ne, dtype=I32), pstarts, 0), axis=2)
    pos = (start_of + rank_t.astype(I32)).T
    nblk = -(-a // EXPERT_BLOCK) + ne
    tok = jnp.broadcast_to(jnp.arange(t, dtype=I32)[:, None], (t, topk))
    row_tok = jnp.zeros((nblk * EXPERT_BLOCK,), I32).at[pos.reshape(a)].set(tok.reshape(a), unique_indices=True)
    n_active = pends[-1] // EXPERT_BLOCK
    blk = jnp.minimum(jnp.arange(nblk, dtype=I32), n_active - 1)
    block_e = jnp.minimum(jnp.sum((pends[None, :] <= (blk * EXPERT_BLOCK)[:, None]).astype(I32), axis=1), ne - 1)
    return row_tok, pos, block_e, n_active.reshape(1)


def _experts_kernel(be_ref, nact_ref, first_ref, nxt_ref, wsl_ref, *refs, layer):
    idx_refs = refs[:ROW_AHEAD + 1]
    (h_ref, wg_hbm, wu_hbm, wd_hbm, y_ref,
     x_s, wg_f, wu_f, wd_f, wg_s, wu_s, wd_s, xsem, wsem) = refs[ROW_AHEAD + 1:]
    idx_ref, idxn_ref = idx_refs[0], idx_refs[ROW_AHEAD]
    i = pl.program_id(0)
    nact = nact_ref[0]
    active = i < nact
    has_next = i + ROW_AHEAD < nact
    slot = lax.rem(i, ROW_SLOTS)
    nslab = wg_s.shape[0] // LANES
    nrows = x_s.shape[1] // nslab

    def row_copy(ids_ref, r, sl):
        src = h_ref.at[pl.ds(pl.multiple_of(ids_ref[0, 0, r], nslab), nslab)]
        return pltpu.make_async_copy(src, x_s.at[sl, pl.ds(r * nslab, nslab)], xsem.at[sl])

    def wait_rows(sl):
        pltpu.make_async_copy(h_ref.at[pl.ds(0, nrows * nslab)], x_s.at[sl], xsem.at[sl]).wait()

    def weight_copies(e, sl):
        return (pltpu.make_async_copy(wg_hbm.at[layer, e], wg_f.at[sl], wsem.at[sl, 0]),
                pltpu.make_async_copy(wu_hbm.at[layer, e], wu_f.at[sl], wsem.at[sl, 1]),
                pltpu.make_async_copy(wd_hbm.at[layer, e], wd_f.at[sl], wsem.at[sl, 2]))

    for a in range(ROW_AHEAD):
        @pl.when(jnp.logical_and(i == 0, a < nact))
        def _(a=a):
            def body(r, carry):
                row_copy(idx_refs[a], r, a).start()
                return carry
            lax.fori_loop(0, nrows, body, 0)

    @pl.when(jnp.logical_and(i == 0, active))
    def _():
        for cp in weight_copies(be_ref[0], 0):
            cp.start(priority=WEIGHT_DMA_PRIORITY)

    @pl.when(jnp.logical_and(active, first_ref[i] == 1))
    def _():
        wsl = wsl_ref[i]

        @pl.when(nxt_ref[i] >= 0)
        def _():
            for cp in weight_copies(nxt_ref[i], 1 - wsl):
                cp.start(priority=WEIGHT_DMA_PRIORITY)

        for cp in weight_copies(be_ref[i], wsl):
            cp.wait()
        wg_s[...] = wg_f[wsl].astype(BF16)
        wu_s[...] = wu_f[wsl].astype(BF16)
        wd_s[...] = wd_f[wsl].astype(BF16)

    def compute(sl, prefetch):
        wait_rows(sl)
        x = _load_slabs(x_s, nrows, nslab, lead=(sl,)).astype(BF16)
        if prefetch:
            for r in range(nrows):
                row_copy(idxn_ref, r, (sl + ROW_AHEAD) % ROW_SLOTS).start()
        hid = _silu(_dot(x, wg_s[...])) * _dot(x, wu_s[...])
        _store_slabs(y_ref, _dot(hid.astype(BF16), wd_s[...]))

    for sl in range(ROW_SLOTS):
        @pl.when(jnp.logical_and(has_next, slot == sl))
        def _(sl=sl):
            compute(sl, True)

        @pl.when(jnp.logical_and(jnp.logical_and(active, jnp.logical_not(has_next)), slot == sl))
        def _(sl=sl):
            compute(sl, False)

    @pl.when(jnp.logical_not(active))
    def _():
        y_ref[...] = jnp.zeros(y_ref.shape, F32)


def _experts(h2, row_tok, block_e, n_active, w_gate, w_up, w_down, layer):
    d, de = w_gate.shape[2], w_gate.shape[3]
    nslab = d // LANES
    nblk = row_tok.shape[0] // EXPERT_BLOCK
    ids = (row_tok * nslab).reshape(nblk, 1, EXPERT_BLOCK)
    blk = jnp.arange(nblk, dtype=I32)
    first = jnp.logical_and(blk < n_active[0],
                            jnp.concatenate([jnp.ones((1,), bool), block_e[1:] != block_e[:-1]]))
    wslot = lax.rem(jnp.cumsum(first.astype(I32)) - 1, 2).astype(I32)
    first_at = jnp.where(first, blk, nblk)
    next_first = lax.cummin(jnp.concatenate([first_at[1:], jnp.full((1,), nblk, I32)]), reverse=True)
    nxt_e = jnp.where(next_first < nblk, block_e[jnp.minimum(next_first, nblk - 1)], -1).astype(I32)
    any_spec = pl.BlockSpec(memory_space=pl.ANY)
    id_specs = [pl.BlockSpec((1, 1, EXPERT_BLOCK), lambda i, *_, a=a: (jnp.minimum(i + a, nblk - 1), 0, 0),
                             memory_space=pltpu.SMEM) for a in range(ROW_AHEAD + 1)]
    grid_spec = pltpu.PrefetchScalarGridSpec(
        num_scalar_prefetch=5,
        grid=(nblk,),
        in_specs=id_specs + [any_spec, any_spec, any_spec, any_spec],
        out_specs=pl.BlockSpec((EXPERT_BLOCK * nslab, LANES), lambda i, *_: (i, 0)),
        scratch_shapes=[pltpu.VMEM((ROW_SLOTS, EXPERT_BLOCK * nslab, LANES), F32),
                        pltpu.VMEM((2, d, de), F32), pltpu.VMEM((2, d, de), F32), pltpu.VMEM((2, de, d), F32),
                        pltpu.VMEM((d, de), BF16), pltpu.VMEM((d, de), BF16), pltpu.VMEM((de, d), BF16),
                        pltpu.SemaphoreType.DMA((ROW_SLOTS,)), pltpu.SemaphoreType.DMA((2, 3))],
    )
    return pl.pallas_call(
        functools.partial(_experts_kernel, layer=layer),
        grid_spec=grid_spec,
        out_shape=jax.ShapeDtypeStruct((nblk * EXPERT_BLOCK * nslab, LANES), F32),
        compiler_params=_params(("arbitrary",)),
        name="moe_experts",
    )(block_e, n_active, first.astype(I32), nxt_e, wslot, *([ids] * (ROW_AHEAD + 1)), h2, w_gate, w_up, w_down)


def _combine_kernel(pos_ref, ys_ref, x1_ref, h2_ref, gw_ref, wsg_ref, wsu_ref, wsd_ref, gpost_ref, gt_ref,
                    o_ref, rows_s, mix_s, sem, *, tt, topk):
    nslab = o_ref.shape[1] // LANES
    for n in range(tt * topk):
        src = ys_ref.at[pl.ds(pl.multiple_of(pos_ref[0, 0, n], nslab), nslab)]
        pltpu.make_async_copy(src, rows_s.at[pl.ds(n * nslab, nslab)], sem).start(priority=n % 2)
    h2 = h2_ref[...].astype(BF16)
    shared = _dot((_silu(_dot(h2, wsg_ref[...])) * _dot(h2, wsu_ref[...])).astype(BF16), wsd_ref[...])
    pltpu.make_async_copy(ys_ref.at[pl.ds(0, tt * topk * nslab)], rows_s, sem).wait()
    for t in range(tt):
        acc = None
        for k in range(topk):
            n = t * topk + k
            term = gw_ref[0, 0, n] * rows_s[n * nslab:(n + 1) * nslab, :]
            acc = term if acc is None else acc + term
        mix_s[t * nslab:(t + 1) * nslab, :] = acc
    f = shared + _load_slabs(mix_s, tt, nslab)
    o_ref[...] = x1_ref[...] + gt_ref[0] * (_rms(f) * gpost_ref[...])


def _combine(ys, pos, x1, h2, gw, wsg, wsu, wsd, gpost, gt, tt):
    t, d = x1.shape
    topk = gw.shape[1]
    de = wsg.shape[1]
    nslab = d // LANES
    scalars = pl.BlockSpec((1, 1, tt * topk), lambda i: (i, 0, 0), memory_space=pltpu.SMEM)
    return pl.pallas_call(
        functools.partial(_combine_kernel, tt=tt, topk=topk),
        grid=(t // tt,),
        in_specs=[scalars,
                  pl.BlockSpec(memory_space=pl.ANY),
                  pl.BlockSpec((tt, d), lambda i: (i, 0)),
                  pl.BlockSpec((tt, d), lambda i: (i, 0)),
                  scalars,
                  pl.BlockSpec((d, de), lambda i: (0, 0)),
                  pl.BlockSpec((d, de), lambda i: (0, 0)),
                  pl.BlockSpec((de, d), lambda i: (0, 0)),
                  pl.BlockSpec((1, d), lambda i: (0, 0)),
                  _mod_spec(gt, t, tt, 0)],
        out_specs=pl.BlockSpec((tt, d), lambda i: (i, 0)),
        out_shape=jax.ShapeDtypeStruct((t, d), F32),
        scratch_shapes=[pltpu.VMEM((tt * topk * nslab, LANES), F32), pltpu.VMEM((tt * nslab, LANES), F32),
                        pltpu.SemaphoreType.DMA(())],
        compiler_params=_params(("arbitrary",)),
        name="moe_combine",
    )((pos * nslab).reshape(t // tt, 1, tt * topk), ys, x1, h2, gw.reshape(t // tt, 1, tt * topk),
      wsg, wsu, wsd, gpost, gt)


def _mod_parts(mod, nb_p, reps):
    parts = jnp.split(mod, 6, axis=-1)
    prompt = [m[:nb_p, None, :] for m in parts]
    sample = [jnp.repeat(m[nb_p:], reps, axis=0)[None] for m in parts]
    return prompt, sample


def kernel(x_prompt, x_sample, cache_k, cache_v, cache_logf, state_conv, state_mlstm_C, state_mlstm_n,
           state_mlstm_m, page_table, c_prompt, c_sample, w_ada, b_ada, g_pre_mix, g_post_mix, g_pre_ffn,
           g_post_ffn, w_in, w_out, conv_dw, conv_db, conv_ln_g, conv_ln_b, mlstm_i_bias, mlstm_f_bias,
           fox_f_bias, w_router, router_bias, w_exp_gate, w_exp_up, w_exp_down, w_sh_gate, w_sh_up, w_sh_down):
    bp, lp, d = x_prompt.shape
    bs, ls, _ = x_sample.shape
    depth = w_ada.shape[0]
    cc = conv_dw.shape[2]
    mh = state_mlstm_C.shape[2]
    mw = mh * HEAD_DIM
    ah = cache_k.shape[3]
    aw = ah * HEAD_DIM
    n_phys, page = cache_k.shape[1], cache_k.shape[2]
    n_pages = page_table.shape[1]
    tp, ts = bp * lp, bs * ls
    n_main = 2 * cc + 4 * mw
    g0 = n_main
    a0 = g0 + 2 * mh
    f0 = a0 + 3 * aw

    xp = x_prompt.reshape(tp, d)
    xs = x_sample.reshape(ts, d)
    mod_all = _ada(jnp.concatenate([c_prompt, c_sample], axis=0), w_ada, b_ada)
    cache_k2 = cache_k.reshape(depth, n_phys, page * ah, HEAD_DIM)
    cache_v2 = cache_v.reshape(depth, n_phys, page * ah, HEAD_DIM)
    cache_lft = jnp.swapaxes(cache_logf, 2, 3)
    zero_conv = jnp.zeros((bp, CONV_W - 1, cc), F32)
    zero_c = jnp.zeros((bp, mh, HEAD_DIM, HEAD_DIM), F32)
    zero_n = jnp.zeros((bp, mh, HEAD_DIM), F32)
    zero_m = jnp.zeros((bp, mh), F32)
    tm_s = ts

    outs = {k: [] for k in ("kp", "vp", "lfp", "ks", "vs", "lfs", "convp", "convs",
                            "cp", "np", "mp", "cs", "ns", "ms")}
    for l in range(depth):
        (sh1p, sc1p, gt1p, sh2p, sc2p, gt2p), (sh1s, sc1s, gt1s, sh2s, sc2s, gt2s) = _mod_parts(mod_all[l], bp, ls)
        w_main = jnp.concatenate([w_in[l][:, :n_main], w_in[l][:, a0:f0]], axis=1).astype(BF16)
        gate_w = jnp.concatenate([w_in[l][:, g0:a0], w_in[l][:, f0:]], axis=1)
        gate_w = jnp.pad(gate_w, ((0, 0), (0, GATE_COLS - gate_w.shape[1]))).astype(BF16)
        gate_bias = jnp.concatenate([mlstm_i_bias[l], mlstm_f_bias[l], fox_f_bias[l]])
        bias_row = jnp.pad(gate_bias, (0, GATE_COLS - gate_bias.shape[0]))[None, :]
        bias_col = gate_bias[:2 * mh, None]
        w_out_b = w_out[l].astype(BF16)
        w_router_t = w_router[l].T
        wsg, wsu, wsd = w_sh_gate[l].astype(BF16), w_sh_up[l].astype(BF16), w_sh_down[l].astype(BF16)
        gpm, gqm = g_pre_mix[l][None, :], g_post_mix[l][None, :]
        gpf, gqf = g_pre_ffn[l][None, :], g_post_ffn[l][None, :]
        dw, db = conv_dw[l], conv_db[l][None, :]
        lng, lnb = conv_ln_g[l][None, :], conv_ln_b[l][None, :]

        groups = []
        for (x, sc1, sh1, tm) in ((xp, sc1p, sh1p, 512), (xs, sc1s, sh1s, tm_s)):
            u = _inproj(x, gpm, sc1, sh1, w_main, tm, 1024)
            gates = _inproj(x, gpm, sc1, sh1, gate_w, tm, GATE_COLS)
            groups.append((u, gates))
        (u_p, gates_p), (u_s, gates_s) = groups

        yc_p, conv_p = _conv(u_p, zero_conv, dw, db, lng, lnb, bp, lp, min(512, lp))
        yc_s, conv_s = _conv(u_s, state_conv[l], dw, db, lng, lnb, bs, ls, ls)

        def gates_t(gates, nb, seq):
            c = MLSTM_CHUNK if seq % MLSTM_CHUNK == 0 else seq
            return gates[:, :2 * mh].reshape(nb, seq // c, c, 2 * mh).transpose(0, 1, 3, 2)

        mcol0 = (2 * cc) // mw
        ym_p, c_p, n_p, m_p = _mlstm(u_p, gates_p, gates_t(gates_p, bp, lp), bias_row, bias_col,
                                     zero_c, zero_n, zero_m, bp, lp, mcol0)
        ym_s, c_s, n_s, m_s = _mlstm(u_s, gates_s, gates_t(gates_s, bs, ls), bias_row, bias_col,
                                     state_mlstm_C[l], state_mlstm_n[l], state_mlstm_m[l], bs, ls, mcol0)

        fb = fox_f_bias[l]
        lf_p = _log_sigmoid(gates_p[:, 2 * mh:2 * mh + ah] + fb).reshape(bp, lp, ah)
        lf_s = _log_sigmoid(gates_s[:, 2 * mh:2 * mh + ah] + fb).reshape(bs, ls, ah)
        fcum = jnp.cumsum(lf_p, axis=1).transpose(0, 2, 1)
        qcol = n_main // HEAD_DIM
        ya_p = _fox_prompt(u_p, fcum[..., None], fcum[:, :, None, :], bp, lp, ah,
                           qcol, qcol + ah, qcol + 2 * ah)
        k_s = u_s[:, n_main + aw:n_main + 2 * aw]
        v_s = u_s[:, n_main + 2 * aw:n_main + 3 * aw]
        past = _past_bias(cache_lft, page_table, l)
        past = past.transpose(0, 2, 1).reshape(bs, n_pages, 1, page * ah)
        ya_s = _fox_sample(u_s[:, n_main:n_main + aw].reshape(ts * ah, HEAD_DIM),
                           k_s.reshape(ts * ah, HEAD_DIM), v_s.reshape(ts * ah, HEAD_DIM),
                           lf_s.reshape(ts * ah, 1), lf_s.reshape(bs, 1, ls * ah),
                           cache_k2, cache_v2, past, page_table, l, ah).reshape(ts, aw)

        x1_p, h2_p, h2slab_p, sco_p = _outproj(yc_p, ym_p, ya_p, xp, w_out_b, gqm, gt1p, gpf, sc2p, sh2p,
                                               w_router_t, 256)
        x1_s, h2_s, h2slab_s, sco_s = _outproj(yc_s, ym_s, ya_s, xs, w_out_b, gqm, gt1s, gpf, sc2s, sh2s,
                                               w_router_t, tm_s)

        idx_t, gw_t, rank_t, counts = _route(jnp.concatenate([sco_p, sco_s], axis=1), router_bias[l])
        row_tok, pos, block_e, n_active = _dispatch(idx_t, rank_t, counts)
        h2_all = jnp.concatenate([h2slab_p, h2slab_s], axis=0)
        ysort = _experts(h2_all, row_tok, block_e, n_active, w_exp_gate, w_exp_up, w_exp_down, l)
        gw = gw_t.T
        xp = _combine(ysort, pos[:tp], x1_p, h2_p, gw[:tp], wsg, wsu, wsd, gqf, gt2p, 128)
        xs = _combine(ysort, pos[tp:], x1_s, h2_s, gw[tp:], wsg, wsu, wsd, gqf, gt2s, min(128, ts))

        outs["kp"].append(u_p[:, n_main + aw:n_main + 2 * aw].reshape(bp, lp, ah, HEAD_DIM))
        outs["vp"].append(u_p[:, n_main + 2 * aw:n_main + 3 * aw].reshape(bp, lp, ah, HEAD_DIM))
        outs["lfp"].append(lf_p)
        outs["ks"].append(k_s.reshape(bs, ls, ah, HEAD_DIM))
        outs["vs"].append(v_s.reshape(bs, ls, ah, HEAD_DIM))
        outs["lfs"].append(lf_s)
        outs["convp"].append(conv_p)
        outs["convs"].append(conv_s)
        outs["cp"].append(c_p)
        outs["np"].append(n_p)
        outs["mp"].append(m_p.reshape(bp, mh))
        outs["cs"].append(c_s)
        outs["ns"].append(n_s)
        outs["ms"].append(m_s.reshape(bs, mh))

    st = {k: jnp.stack(v) for k, v in outs.items()}
    return (xp.reshape(bp, lp, d), xs.reshape(bs, ls, d),
            st["kp"], st["vp"], st["lfp"], st["ks"], st["vs"], st["lfs"],
            st["convp"], st["convs"], st["cp"], st["np"], st["mp"], st["cs"], st["ns"], st["ms"])
```

```python
import functools

import jax
import jax.numpy as jnp
from jax import lax
from jax.experimental import pallas as pl
from jax.experimental.pallas import tpu as pltpu

F32 = jnp.float32
BF16 = jnp.bfloat16
I32 = jnp.int32
HI = lax.Precision.HIGHEST

HEAD_DIM = 128
CONV_W = 31
MLSTM_CHUNK = 64
N_EXPERT_GROUPS = 8
TOPK_GROUPS = 4
TOP_K = 8
ROUTED_SCALE = 2.5
NORM_EPS = 1e-6

LANES = 128
GATE_COLS = 128
EXPERT_BLOCK = 256
VMEM_LIMIT = 56 * 1024 * 1024
WEIGHT_DMA_PRIORITY = 1
ROW_AHEAD = 3
ROW_SLOTS = ROW_AHEAD + 1


def _params(sem):
    return pltpu.CompilerParams(dimension_semantics=sem, vmem_limit_bytes=VMEM_LIMIT)


def _log_sigmoid(x):
    return jnp.minimum(x, 0.0) - jnp.log1p(jnp.exp(-jnp.abs(x)))


def _silu(x):
    return x * jax.nn.sigmoid(x)


def _rms(x):
    return x * lax.rsqrt(jnp.mean(x * x, axis=-1, keepdims=True) + NORM_EPS)


def _dot(a, b):
    return jnp.dot(a, b, preferred_element_type=F32)


def _dot_nt(a, b):
    return lax.dot_general(a, b, (((1,), (1,)), ((), ())), preferred_element_type=F32)


def _dot_tn(a, b):
    return lax.dot_general(a, b, (((0,), (0,)), ((), ())), preferred_element_type=F32)


def _dot_hi(a, b):
    return jnp.dot(a, b, preferred_element_type=F32, precision=HI)


def _split3(a):
    hi = a.astype(BF16)
    rest = a - hi.astype(F32)
    mid = rest.astype(BF16)
    lo = (rest - mid.astype(F32)).astype(BF16)
    return hi, mid, lo


def _dot_ones(a, ones):
    ob = ones.astype(BF16)
    hi, mid, lo = _split3(a)
    return _dot(hi, ob) + _dot(mid, ob) + _dot(lo, ob)


def _ones_dot(ones, a):
    ob = ones.astype(BF16)
    hi, mid, lo = _split3(a)
    return _dot(ob, hi) + _dot(ob, mid) + _dot(ob, lo)


def _lane_tile(t, cap):
    best = LANES
    for k in range(1, cap // LANES + 1):
        if t % (k * LANES) == 0:
            best = k * LANES
    return best


def _store_slabs(ref, x):
    rows, d = x.shape
    nslab = d // LANES
    for j in range(nslab):
        ref[pl.ds(j, rows, stride=nslab), :] = x[:, j * LANES:(j + 1) * LANES]


def _load_slabs(ref, rows, nslab, lead=()):
    return jnp.concatenate([ref[lead + (pl.ds(j, rows, stride=nslab), slice(None))] for j in range(nslab)],
                           axis=1)


def _ada_kernel(c_ref, w_ref, b_ref, o_ref):
    s = _silu(c_ref[...]).astype(BF16)
    o_ref[0] = _dot(s, w_ref[0].astype(BF16)) + b_ref[0]


def _ada(c_all, w_ada, b_ada):
    depth, d, n = w_ada.shape
    r = c_all.shape[0]
    tn = 1024
    return pl.pallas_call(
        _ada_kernel,
        grid=(depth, n // tn),
        in_specs=[pl.BlockSpec((r, d), lambda l, j: (0, 0)),
                  pl.BlockSpec((1, d, tn), lambda l, j: (l, 0, j)),
                  pl.BlockSpec((1, 1, tn), lambda l, j: (l, 0, j))],
        out_specs=pl.BlockSpec((1, r, tn), lambda l, j: (l, 0, j)),
        out_shape=jax.ShapeDtypeStruct((depth, r, n), F32),
        compiler_params=_params(("arbitrary", "arbitrary")),
        name="ada",
    )(c_all, w_ada, b_ada.reshape(depth, 1, n))


def _mod_spec(mod, t, tm, row_axis):
    ngrp, r, d = mod.shape
    if r == 1:
        tiles_per_grp = (t // ngrp) // tm
        return pl.BlockSpec((1, 1, d), lambda *ix: (ix[row_axis] // tiles_per_grp, 0, 0))
    return pl.BlockSpec((1, tm, d), lambda *ix: (0, ix[row_axis], 0))


def _inproj_kernel(x_ref, g_ref, sc_ref, sh_ref, w_ref, o_ref):
    h = (_rms(x_ref[...]) * g_ref[...]) * (1.0 + sc_ref[0]) + sh_ref[0]
    o_ref[...] = _dot(h.astype(BF16), w_ref[...])


def _inproj(x, g, sc, sh, w, tm, tn):
    t, d = x.shape
    n = w.shape[1]
    mod_spec = _mod_spec(sc, t, tm, 1)
    return pl.pallas_call(
        _inproj_kernel,
        grid=(n // tn, t // tm),
        in_specs=[pl.BlockSpec((tm, d), lambda j, i: (i, 0)),
                  pl.BlockSpec((1, d), lambda j, i: (0, 0)),
                  mod_spec, mod_spec,
                  pl.BlockSpec((d, tn), lambda j, i: (0, j))],
        out_specs=pl.BlockSpec((tm, tn), lambda j, i: (i, j)),
        out_shape=jax.ShapeDtypeStruct((t, n), F32),
        compiler_params=_params(("arbitrary", "arbitrary")),
        name="inproj",
    )(x, g, sc, sh, w)


def _conv_kernel(a_ref, g_ref, buf_ref, dw_ref, db_ref, lg_ref, lb_ref, y_ref, st_ref, full_ref,
                 *, lc, nchunks):
    halo = CONV_W + 1
    j = pl.program_id(1)

    @pl.when(j == 0)
    def _():
        full_ref[0:2, :] = jnp.zeros((2, full_ref.shape[1]), F32)
        full_ref[2:halo, :] = buf_ref[0]

    u = a_ref[...] * jax.nn.sigmoid(g_ref[...])
    full_ref[halo:halo + lc, :] = u
    acc = jnp.zeros_like(u)
    for w in range(CONV_W):
        acc = acc + full_ref[2 + w:2 + w + lc, :] * dw_ref[w:w + 1, :]
    acc = acc + db_ref[...]
    mu = jnp.mean(acc, axis=-1, keepdims=True)
    cen = acc - mu
    var = jnp.mean(cen * cen, axis=-1, keepdims=True)
    y = cen * lax.rsqrt(var + NORM_EPS) * lg_ref[...] + lb_ref[...]
    y_ref[...] = _silu(y)

    @pl.when(j == nchunks - 1)
    def _():
        st_ref[0] = full_ref[2 + lc:halo + lc, :]

    if nchunks > 1:
        full_ref[0:halo, :] = full_ref[lc:lc + halo, :]


def _conv(u, buf, dw, db, lg, lb, nb, seq, lc):
    c = buf.shape[2]
    nchunks = seq // lc
    vec = pl.BlockSpec((1, c), lambda b, j: (0, 0))
    return pl.pallas_call(
        functools.partial(_conv_kernel, lc=lc, nchunks=nchunks),
        grid=(nb, nchunks),
        in_specs=[pl.BlockSpec((lc, c), lambda b, j: (b * nchunks + j, 0)),
                  pl.BlockSpec((lc, c), lambda b, j: (b * nchunks + j, 1)),
                  pl.BlockSpec((1, CONV_W - 1, c), lambda b, j: (b, 0, 0)),
                  pl.BlockSpec((CONV_W, c), lambda b, j: (0, 0)),
                  vec, vec, vec],
        out_specs=[pl.BlockSpec((lc, c), lambda b, j: (b * nchunks + j, 0)),
                   pl.BlockSpec((1, CONV_W - 1, c), lambda b, j: (b, 0, 0))],
        out_shape=[jax.ShapeDtypeStruct((nb * seq, c), F32),
                   jax.ShapeDtypeStruct((nb, CONV_W - 1, c), F32)],
        scratch_shapes=[pltpu.VMEM((CONV_W + 1 + lc, c), F32)],
        compiler_params=_params(("arbitrary", "arbitrary")),
        name="conv",
    )(u, u, buf, dw, db, lg, lb)


def _mlstm_kernel(q_ref, k_ref, v_ref, o_ref, gc_ref, gr_ref, bc_ref, br_ref, c0_ref, n0_ref, m0_ref,
                  y_ref, cout_ref, nout_ref, mout_ref, c_s, n_s, m_s, *, c, nc, nh, nbs):
    j = pl.program_id(1)

    @pl.when(j == 0)
    def _():
        c_s[...] = c0_ref[...]
        n_s[...] = n0_ref[...]
        for b in range(nbs):
            for h in range(nh):
                m_s[b, h] = jnp.broadcast_to(m0_ref[b][:, h:h + 1], m_s.shape[2:])

    for b in range(nbs):
        _mlstm_chunk(b, q_ref, k_ref, v_ref, o_ref, gc_ref, gr_ref, bc_ref, br_ref, y_ref, c_s, n_s, m_s,
                     c=c, nh=nh)

    @pl.when(j == nc - 1)
    def _():
        cout_ref[...] = c_s[...]
        nout_ref[...] = n_s[...]
        lane = lax.broadcasted_iota(I32, (1, nh), 1)
        for b in range(nbs):
            m_out = jnp.zeros((1, nh), F32)
            for h in range(nh):
                m_out = jnp.where(lane == h, m_s[b, h][0:1, 0:1], m_out)
            mout_ref[b] = m_out


def _mlstm_chunk(b, q_ref, k_ref, v_ref, o_ref, gc_ref, gr_ref, bc_ref, br_ref, y_ref, c_s, n_s, m_s, *, c, nh):
    scale = HEAD_DIM ** -0.5
    mm = BF16 if c % 16 == 0 else F32
    gcol = gc_ref[b] + bc_ref[...]
    grow = gr_ref[b, 0] + br_ref[...]
    lf_col = _log_sigmoid(gcol)
    lf_row = _log_sigmoid(grow)
    row = lax.broadcasted_iota(I32, (c, c), 0)
    col = lax.broadcasted_iota(I32, (c, c), 1)
    causal = col <= row
    if mm == BF16:
        b_col_all = _ones_dot(causal, lf_col)
        b_row_all = _dot_ones(lf_row, row <= col)
    else:
        b_col_all = _dot_hi(causal.astype(F32), lf_col)
        b_row_all = _dot_hi(lf_row, (row <= col).astype(F32))

    for h in range(nh):
        sl = slice(h * HEAD_DIM, (h + 1) * HEAD_DIM)
        q = q_ref[b, :, sl]
        k = k_ref[b, :, sl] * scale
        v = v_ref[b, :, sl]
        b_c = b_col_all[:, nh + h:nh + h + 1]
        b_r = b_row_all[nh + h:nh + h + 1, :]
        ig_r = grow[h:h + 1, :]
        ig_c = gcol[:, h:h + 1]
        m_prev = m_s[b, h][0:1, 0:1]
        dmat = jnp.where(causal, b_c - b_r + ig_r, -jnp.inf)
        inter = b_c + m_prev
        m_t = jnp.maximum(inter, jnp.max(dmat, axis=1, keepdims=True))
        dw = jnp.exp(dmat - m_t)
        iw = jnp.exp(inter - m_t)
        qm, km, vm = q.astype(mm), k.astype(mm), v.astype(mm)
        s = _dot_nt(qm, km) * dw
        c_h = c_s[b, h]
        n_h = n_s[b, h:h + 1, :]
        num = _dot(s.astype(mm), vm) + iw * _dot(qm, c_h.astype(mm))
        den = jnp.sum(s, axis=1, keepdims=True) + iw * jnp.sum(q * n_h, axis=1, keepdims=True)
        hid = num / jnp.maximum(jnp.abs(den), jnp.exp(-m_t))
        y_ref[b, :, sl] = jax.nn.sigmoid(o_ref[b, :, sl]) * hid
        m_last = m_t[c - 1:c, :]
        w_end = jnp.exp(b_c[c - 1:c, :] - b_c + ig_c - m_last)
        f_end = iw[c - 1:c, :]
        kw = k * w_end
        c_s[b, h] = f_end * c_h + _dot_tn(kw.astype(mm), vm)
        n_s[b, h:h + 1, :] = f_end * n_h + jnp.sum(kw, axis=0, keepdims=True)
        m_s[b, h] = jnp.broadcast_to(m_last, m_s.shape[2:])


def _mlstm(u, gates, gates_t, bias_row, bias_col, c0, n0, m0, nb, seq, col0):
    nh = c0.shape[1]
    w = nh * HEAD_DIM
    c = MLSTM_CHUNK if seq % MLSTM_CHUNK == 0 else seq
    nc = seq // c
    nbs = 4 if nb % 4 == 0 else 1
    u3 = u.reshape(nb, seq, u.shape[1])

    def ublk(off):
        return pl.BlockSpec((nbs, c, w), lambda g, j: (g, j, col0 + off))

    y, c_out, n_out, m_out = pl.pallas_call(
        functools.partial(_mlstm_kernel, c=c, nc=nc, nh=nh, nbs=nbs),
        grid=(nb // nbs, nc),
        in_specs=[ublk(0), ublk(1), ublk(2), ublk(3),
                  pl.BlockSpec((nbs, c, GATE_COLS), lambda g, j: (g, j, 0)),
                  pl.BlockSpec((nbs, 1, 2 * nh, c), lambda g, j: (g, j, 0, 0)),
                  pl.BlockSpec((1, GATE_COLS), lambda g, j: (0, 0)),
                  pl.BlockSpec((2 * nh, 1), lambda g, j: (0, 0)),
                  pl.BlockSpec((nbs, nh, HEAD_DIM, HEAD_DIM), lambda g, j: (g, 0, 0, 0)),
                  pl.BlockSpec((nbs, nh, HEAD_DIM), lambda g, j: (g, 0, 0)),
                  pl.BlockSpec((nbs, 1, nh), lambda g, j: (g, 0, 0))],
        out_specs=[pl.BlockSpec((nbs, c, w), lambda g, j: (g, j, 0)),
                   pl.BlockSpec((nbs, nh, HEAD_DIM, HEAD_DIM), lambda g, j: (g, 0, 0, 0)),
                   pl.BlockSpec((nbs, nh, HEAD_DIM), lambda g, j: (g, 0, 0)),
                   pl.BlockSpec((nbs, 1, nh), lambda g, j: (g, 0, 0))],
        out_shape=[jax.ShapeDtypeStruct((nb, seq, w), F32),
                   jax.ShapeDtypeStruct((nb, nh, HEAD_DIM, HEAD_DIM), F32),
                   jax.ShapeDtypeStruct((nb, nh, HEAD_DIM), F32),
                   jax.ShapeDtypeStruct((nb, 1, nh), F32)],
        scratch_shapes=[pltpu.VMEM((nbs, nh, HEAD_DIM, HEAD_DIM), F32),
                        pltpu.VMEM((nbs, nh, HEAD_DIM), F32),
                        pltpu.VMEM((nbs, nh, 8, LANES), F32)],
        compiler_params=_params(("arbitrary", "arbitrary")),
        name="mlstm",
    )(u3, u3, u3, u3, gates.reshape(nb, seq, GATE_COLS), gates_t, bias_row, bias_col, c0, n0,
      m0.reshape(nb, 1, nh))
    return y.reshape(nb * seq, w), c_out, n_out, m_out


def _fox_prompt_kernel(q_ref, k_ref, v_ref, fc_ref, fr_ref, o_ref, *, seq, tq):
    scale = HEAD_DIM ** -0.5
    kb = k_ref[...].astype(BF16)
    vb = v_ref[...].astype(BF16)
    f_row = fr_ref[0, 0]
    for qi in range(seq // tq):
        rows = slice(qi * tq, (qi + 1) * tq)
        kend = (qi + 1) * tq
        q = (q_ref[rows, :] * scale).astype(BF16)
        s = _dot_nt(q, kb[:kend]) + fc_ref[0, 0][rows, :] - f_row[:, :kend]
        qpos = lax.broadcasted_iota(I32, (tq, kend), 0) + qi * tq
        kpos = lax.broadcasted_iota(I32, (tq, kend), 1)
        s = jnp.where(kpos <= qpos, s, -jnp.inf)
        p = jnp.exp(s - jnp.max(s, axis=1, keepdims=True))
        denom = jnp.sum(p, axis=1, keepdims=True)
        o_ref[rows, :] = _dot(p.astype(BF16), vb[:kend]) / denom


def _fox_prompt(u, f_col, f_row, nb, seq, nh, qcol, kcol, vcol):
    tq = min(256, seq)
    return pl.pallas_call(
        functools.partial(_fox_prompt_kernel, seq=seq, tq=tq),
        grid=(nb, nh),
        in_specs=[pl.BlockSpec((seq, HEAD_DIM), lambda b, h: (b, qcol + h)),
                  pl.BlockSpec((seq, HEAD_DIM), lambda b, h: (b, kcol + h)),
                  pl.BlockSpec((seq, HEAD_DIM), lambda b, h: (b, vcol + h)),
                  pl.BlockSpec((1, 1, seq, 1), lambda b, h: (b, h, 0, 0)),
                  pl.BlockSpec((1, 1, 1, seq), lambda b, h: (b, h, 0, 0))],
        out_specs=pl.BlockSpec((seq, HEAD_DIM), lambda b, h: (b, h)),
        out_shape=jax.ShapeDtypeStruct((nb * seq, nh * HEAD_DIM), F32),
        compiler_params=_params(("arbitrary", "arbitrary")),
        name="fox_prompt",
    )(u, u, u, f_col, f_row)


def _pages_per_step(n_pages):
    return 8 if n_pages % 8 == 0 else 1


def _past_bias_kernel(pt_ref, *rest, npg):
    lf_refs = rest[:npg]
    o_ref, carry_s = rest[npg:]
    page = lf_refs[0].shape[3]

    @pl.when(pl.program_id(1) == 0)
    def _():
        carry_s[...] = jnp.zeros(carry_s.shape, F32)

    later = lax.broadcasted_iota(I32, (page, page), 0) > lax.broadcasted_iota(I32, (page, page), 1)
    carry = carry_s[...]
    for r in reversed(range(npg)):
        lft = lf_refs[r][0, 0]
        o_ref[0, :, r * page:(r + 1) * page] = carry + _dot_ones(lft, later)
        carry = carry + jnp.sum(lft, axis=1, keepdims=True)
    carry_s[...] = carry


def _past_bias(cache_lft, page_table, layer):
    nb, n_pages = page_table.shape
    nh, page = cache_lft.shape[2], cache_lft.shape[3]
    npg = 64 if n_pages % 64 == 0 else _pages_per_step(n_pages)
    ngrp = n_pages // npg
    specs = [pl.BlockSpec((1, 1, nh, page),
                          lambda b, g, pt, r=r: (layer, pt[b, (ngrp - 1 - g) * npg + r], 0, 0))
             for r in range(npg)]
    grid_spec = pltpu.PrefetchScalarGridSpec(
        num_scalar_prefetch=1,
        grid=(nb, ngrp),
        in_specs=specs,
        out_specs=pl.BlockSpec((1, nh, npg * page), lambda b, g, pt: (b, 0, ngrp - 1 - g)),
        scratch_shapes=[pltpu.VMEM((nh, 1), F32)],
    )
    return pl.pallas_call(
        functools.partial(_past_bias_kernel, npg=npg),
        grid_spec=grid_spec,
        out_shape=jax.ShapeDtypeStruct((nb, nh, n_pages * page), F32),
        compiler_params=_params(("arbitrary", "arbitrary")),
        name="fox_past_bias",
    )(page_table, *([cache_lft] * npg))


def _fox_sample_kernel(pt_ref, q_ref, kn_ref, vn_ref, lfc_ref, lfr_ref, *rest, npg, ngrp, nh):
    k_refs = rest[0:npg]
    v_refs = rest[npg:2 * npg]
    b_refs = rest[2 * npg:3 * npg]
    o_ref, m_s, l_s, acc_s = rest[3 * npg:]
    g = pl.program_id(1)
    scale = HEAD_DIM ** -0.5
    nr = q_ref.shape[0]
    ncol = k_refs[0].shape[2]
    hshift = nh.bit_length() - 1
    qb = (q_ref[...] * scale).astype(BF16)

    ri = lax.broadcasted_iota(I32, (nr, nr), 0)
    ci = lax.broadcasted_iota(I32, (nr, nr), 1)
    same_head = (ri & (nh - 1)) == (ci & (nh - 1))
    col_le_row = jnp.logical_and(same_head, (ci >> hshift) <= (ri >> hshift))
    row_le_col = jnp.logical_and(same_head, (ri >> hshift) <= (ci >> hshift))
    f_col = jnp.sum(jnp.where(col_le_row, lfr_ref[0], 0.0), axis=1, keepdims=True)

    @pl.when(g == 0)
    def _():
        f_row = jnp.sum(jnp.where(row_le_col, lfc_ref[...], 0.0), axis=0, keepdims=True)
        s = _dot_nt(qb, kn_ref[...].astype(BF16)) + f_col - f_row
        s = jnp.where(col_le_row, s, -jnp.inf)
        m = jnp.max(s, axis=1, keepdims=True)
        p = jnp.exp(s - m)
        m_s[...] = m
        l_s[...] = jnp.sum(p, axis=1, keepdims=True)
        acc_s[...] = _dot(p.astype(BF16), vn_ref[...].astype(BF16))

    rr = lax.broadcasted_iota(I32, (nr, ncol), 0)
    cc = lax.broadcasted_iota(I32, (nr, ncol), 1)
    base = f_col + jnp.where((rr & (nh - 1)) == (cc & (nh - 1)), 0.0, -jnp.inf)
    s = jnp.concatenate([_dot_nt(qb, k_refs[r][0, 0].astype(BF16)) + (base + b_refs[r][0, 0])
                         for r in range(npg)], axis=1)
    m_old = m_s[...]
    m_new = jnp.maximum(m_old, jnp.max(s, axis=1, keepdims=True))
    alpha = jnp.exp(m_old - m_new)
    p = jnp.exp(s - m_new)
    l_s[...] = alpha * l_s[...] + jnp.sum(p, axis=1, keepdims=True)
    pv = _dot(p[:, 0:ncol].astype(BF16), v_refs[0][0, 0].astype(BF16))
    for r in range(1, npg):
        pv = pv + _dot(p[:, r * ncol:(r + 1) * ncol].astype(BF16), v_refs[r][0, 0].astype(BF16))
    acc_s[...] = alpha * acc_s[...] + pv
    m_s[...] = m_new

    @pl.when(g == ngrp - 1)
    def _():
        o_ref[...] = acc_s[...] / l_s[...]


def _fox_sample(q2, kn2, vn2, lf_col, lf_row, cache_k2, cache_v2, bias, page_table, layer, nh):
    nb, n_pages = page_table.shape
    nr = q2.shape[0] // nb
    ncol = cache_k2.shape[2]
    npg = _pages_per_step(n_pages)
    ngrp = n_pages // npg
    assert nh & (nh - 1) == 0

    kv_specs = [pl.BlockSpec((1, 1, ncol, HEAD_DIM), lambda b, g, pt, r=r: (layer, pt[b, g * npg + r], 0, 0))
                for r in range(npg)]
    b_specs = [pl.BlockSpec((1, 1, 1, ncol), lambda b, g, pt, r=r: (b, g * npg + r, 0, 0)) for r in range(npg)]
    rows = pl.BlockSpec((nr, HEAD_DIM), lambda b, g, pt: (b, 0))
    grid_spec = pltpu.PrefetchScalarGridSpec(
        num_scalar_prefetch=1,
        grid=(nb, ngrp),
        in_specs=[rows, rows, rows,
                  pl.BlockSpec((nr, 1), lambda b, g, pt: (b, 0)),
                  pl.BlockSpec((1, 1, nr), lambda b, g, pt: (b, 0, 0))] + kv_specs + kv_specs + b_specs,
        out_specs=rows,
        scratch_shapes=[pltpu.VMEM((nr, 1), F32), pltpu.VMEM((nr, 1), F32), pltpu.VMEM((nr, HEAD_DIM), F32)],
    )
    return pl.pallas_call(
        functools.partial(_fox_sample_kernel, npg=npg, ngrp=ngrp, nh=nh),
        grid_spec=grid_spec,
        out_shape=jax.ShapeDtypeStruct(q2.shape, F32),
        compiler_params=_params(("arbitrary", "arbitrary")),
        name="fox_sample",
    )(page_table, q2, kn2, vn2, lf_col, lf_row, *([cache_k2] * npg), *([cache_v2] * npg), *([bias] * npg))


def _outproj_kernel(yc_ref, ym_ref, ya_ref, x_ref, wc_ref, wm_ref, wa_ref, gpost_ref, gt_ref, gpre_ref,
                    sc_ref, sh_ref, wrt_ref, x1_ref, h2_ref, h2s_ref, sco_ref):
    mix = (_dot(yc_ref[...].astype(BF16), wc_ref[...]) + _dot(ym_ref[...].astype(BF16), wm_ref[...])
           + _dot(ya_ref[...].astype(BF16), wa_ref[...]))
    x1 = x_ref[...] + gt_ref[0] * (_rms(mix) * gpost_ref[...])
    x1_ref[...] = x1
    h2 = (_rms(x1) * gpre_ref[...]) * (1.0 + sc_ref[0]) + sh_ref[0]
    h2_ref[...] = h2
    _store_slabs(h2s_ref, h2)
    w_hi, w_lo, _ = _split3(wrt_ref[...])
    h_hi, h_lo, _ = _split3(h2)
    logits_t = _dot_nt(w_hi, h_hi) + (_dot_nt(w_hi, h_lo) + _dot_nt(w_lo, h_hi))
    sco_ref[...] = jax.nn.sigmoid(logits_t)


def _outproj(yc, ym, ya, x, w_out, gpost, gt, gpre, sc, sh, w_router_t, tm):
    t, d = x.shape
    cw, mw, aw = yc.shape[1], ym.shape[1], ya.shape[1]
    ne = w_router_t.shape[0]
    mod_spec = _mod_spec(sc, t, tm, 0)
    vec = pl.BlockSpec((1, d), lambda i: (0, 0))
    return pl.pallas_call(
        _outproj_kernel,
        grid=(t // tm,),
        in_specs=[pl.BlockSpec((tm, cw), lambda i: (i, 0)),
                  pl.BlockSpec((tm, mw), lambda i: (i, 0)),
                  pl.BlockSpec((tm, aw), lambda i: (i, 0)),
                  pl.BlockSpec((tm, d), lambda i: (i, 0)),
                  pl.BlockSpec((cw, d), lambda i: (0, 0)),
                  pl.BlockSpec((mw, d), lambda i: (cw // mw, 0)),
                  pl.BlockSpec((aw, d), lambda i: ((cw + mw) // aw, 0)),
                  vec, mod_spec, vec, mod_spec, mod_spec,
                  pl.BlockSpec((ne, d), lambda i: (0, 0))],
        out_specs=[pl.BlockSpec((tm, d), lambda i: (i, 0)),
                   pl.BlockSpec((tm, d), lambda i: (i, 0)),
                   pl.BlockSpec((tm * (d // LANES), LANES), lambda i: (i, 0)),
                   pl.BlockSpec((ne, tm), lambda i: (0, i))],
        out_shape=[jax.ShapeDtypeStruct((t, d), F32),
                   jax.ShapeDtypeStruct((t, d), F32),
                   jax.ShapeDtypeStruct((t * (d // LANES), LANES), F32),
                   jax.ShapeDtypeStruct((ne, t), F32)],
        compiler_params=_params(("arbitrary",)),
        name="outproj",
    )(yc, ym, ya, x, w_out, w_out, w_out, gpost, gt, gpre, sc, sh, w_router_t)


def _route_kernel(s_ref, b_ref, idx_ref, w_ref, rank_ref, cnt_ref, carry_s, *, ne, ng, tt, ntiles):
    i = pl.program_id(0)
    gs = ne // ng

    @pl.when(i == 0)
    def _():
        carry_s[...] = jnp.zeros(carry_s.shape, F32)

    def first_max(x, ids, n):
        m = jnp.max(x, axis=0, keepdims=True)
        return m, jnp.min(jnp.where(x == m, ids, float(n)), axis=0, keepdims=True)

    s = s_ref[...]
    sel = s + b_ref[...]
    sub = lax.broadcasted_iota(I32, (gs, tt), 0).astype(F32)
    grow = lax.broadcasted_iota(I32, (ng, tt), 0).astype(F32)
    gscore = jnp.zeros((ng, tt), F32)
    for g in range(ng):
        blk = sel[g * gs:(g + 1) * gs, :]
        m1, i1 = first_max(blk, sub, gs)
        m2 = jnp.max(jnp.where(sub == i1, -jnp.inf, blk), axis=0, keepdims=True)
        gscore = jnp.where(grow == g, m1 + m2, gscore)
    gsel = jnp.zeros((ng, tt), F32)
    for _ in range(TOPK_GROUPS):
        _, ig = first_max(gscore, grow, ng)
        hit = grow == ig
        gsel = jnp.where(hit, 1.0, gsel)
        gscore = jnp.where(hit, -jnp.inf, gscore)
    cand = jnp.concatenate([jnp.where(gsel[g:g + 1, :] > 0.0, sel[g * gs:(g + 1) * gs, :], -jnp.inf)
                            for g in range(ng)], axis=0)

    erow = lax.broadcasted_iota(I32, (ne, tt), 0).astype(F32)
    member = jnp.zeros((ne, tt), F32)
    picks, weights = [], []
    for _ in range(TOP_K):
        _, ie = first_max(cand, erow, ne)
        hit = erow == ie
        picks.append(ie)
        weights.append(jnp.sum(jnp.where(hit, s, 0.0), axis=0, keepdims=True))
        member = jnp.where(hit, 1.0, member)
        cand = jnp.where(hit, -jnp.inf, cand)
    wsum = weights[0]
    for k in range(1, TOP_K):
        wsum = wsum + weights[k]

    upto = lax.broadcasted_iota(I32, (tt, tt), 0) <= lax.broadcasted_iota(I32, (tt, tt), 1)
    incl = _dot(member.astype(BF16), upto.astype(BF16)) + carry_s[...]
    rank = incl - member
    carry_s[...] = incl[:, tt - 1:tt]

    krow = lax.broadcasted_iota(I32, (TOP_K, tt), 0)
    idx_out = jnp.zeros((TOP_K, tt), F32)
    w_out = jnp.zeros((TOP_K, tt), F32)
    rank_out = jnp.zeros((TOP_K, tt), F32)
    for k in range(TOP_K):
        idx_out = jnp.where(krow == k, picks[k], idx_out)
        w_out = jnp.where(krow == k, weights[k] / wsum * ROUTED_SCALE, w_out)
        rk = jnp.sum(jnp.where(erow == picks[k], rank, 0.0), axis=0, keepdims=True)
        rank_out = jnp.where(krow == k, rk, rank_out)
    idx_ref[...] = idx_out.astype(I32)
    w_ref[...] = w_out
    rank_ref[...] = rank_out

    @pl.when(i == ntiles - 1)
    def _():
        cnt_ref[...] = carry_s[...]


def _route(scores_t, router_bias):
    ne, t = scores_t.shape
    tt = _lane_tile(t, 768)
    ntiles = t // tt
    tok = pl.BlockSpec((TOP_K, tt), lambda i: (0, i))
    return pl.pallas_call(
        functools.partial(_route_kernel, ne=ne, ng=N_EXPERT_GROUPS, tt=tt, ntiles=ntiles),
        grid=(ntiles,),
        in_specs=[pl.BlockSpec((ne, tt), lambda i: (0, i)),
                  pl.BlockSpec((ne, 1), lambda i: (0, 0))],
        out_specs=[tok, tok, tok, pl.BlockSpec((ne, 1), lambda i: (0, 0))],
        out_shape=[jax.ShapeDtypeStruct((TOP_K, t), I32),
                   jax.ShapeDtypeStruct((TOP_K, t), F32),
                   jax.ShapeDtypeStruct((TOP_K, t), F32),
                   jax.ShapeDtypeStruct((ne, 1), F32)],
        scratch_shapes=[pltpu.VMEM((ne, 1), F32)],
        compiler_params=_params(("arbitrary",)),
        name="moe_route",
    )(scores_t, router_bias.astype(F32)[:, None])


def _dispatch(idx_t, rank_t, counts):
    topk, t = idx_t.shape
    ne = counts.shape[0]
    a = t * topk
    counts = counts.reshape(ne).astype(I32)
    padded = (counts + EXPERT_BLOCK - 1) // EXPERT_BLOCK * EXPERT_BLOCK
    pends = jnp.cumsum(padded)
    pstarts = pends - padded
    start_of = jnp.sum(jnp.where(idx_t[:, :, None] == jnp.arange(ne, dtype=I32), pstarts, 0), axis=2)
    pos = (start_of + rank_t.astype(I32)).T
    nblk = -(-a // EXPERT_BLOCK) + ne
    tok = jnp.broadcast_to(jnp.arange(t, dtype=I32)[:, None], (t, topk))
    row_tok = jnp.zeros((nblk * EXPERT_BLOCK,), I32).at[pos.reshape(a)].set(tok.reshape(a), unique_indices=True)
    n_active = pends[-1] // EXPERT_BLOCK
    blk = jnp.minimum(jnp.arange(nblk, dtype=I32), n_active - 1)
    block_e = jnp.minimum(jnp.sum((pends[None, :] <= (blk * EXPERT_BLOCK)[:, None]).astype(I32), axis=1), ne - 1)
    return row_tok, pos, block_e, n_active.reshape(1)


def _experts_kernel(be_ref, nact_ref, first_ref, nxt_ref, wsl_ref, *refs, layer):
    idx_refs = refs[:ROW_AHEAD + 1]
    (h_ref, wg_hbm, wu_hbm, wd_hbm, y_ref,
     x_s, wg_f, wu_f, wd_f, wg_s, wu_s, wd_s, xsem, wsem) = refs[ROW_AHEAD + 1:]
    idx_ref, idxn_ref = idx_refs[0], idx_refs[ROW_AHEAD]
    i = pl.program_id(0)
    nact = nact_ref[0]
    active = i < nact
    has_next = i + ROW_AHEAD < nact
    slot = lax.rem(i, ROW_SLOTS)
    nslab = wg_s.shape[0] // LANES
    nrows = x_s.shape[1] // nslab

    def row_copy(ids_ref, r, sl):
        src = h_ref.at[pl.ds(pl.multiple_of(ids_ref[0, 0, r], nslab), nslab)]
        return pltpu.make_async_copy(src, x_s.at[sl, pl.ds(r * nslab, nslab)], xsem.at[sl])

    def wait_rows(sl):
        pltpu.make_async_copy(h_ref.at[pl.ds(0, nrows * nslab)], x_s.at[sl], xsem.at[sl]).wait()

    def weight_copies(e, sl):
        return (pltpu.make_async_copy(wg_hbm.at[layer, e], wg_f.at[sl], wsem.at[sl, 0]),
                pltpu.make_async_copy(wu_hbm.at[layer, e], wu_f.at[sl], wsem.at[sl, 1]),
                pltpu.make_async_copy(wd_hbm.at[layer, e], wd_f.at[sl], wsem.at[sl, 2]))

    for a in range(ROW_AHEAD):
        @pl.when(jnp.logical_and(i == 0, a < nact))
        def _(a=a):
            def body(r, carry):
                row_copy(idx_refs[a], r, a).start()
                return carry
            lax.fori_loop(0, nrows, body, 0)

    @pl.when(jnp.logical_and(i == 0, active))
    def _():
        for cp in weight_copies(be_ref[0], 0):
            cp.start(priority=WEIGHT_DMA_PRIORITY)

    @pl.when(jnp.logical_and(active, first_ref[i] == 1))
    def _():
        wsl = wsl_ref[i]

        @pl.when(nxt_ref[i] >= 0)
        def _():
            for cp in weight_copies(nxt_ref[i], 1 - wsl):
                cp.start(priority=WEIGHT_DMA_PRIORITY)

        for cp in weight_copies(be_ref[i], wsl):
            cp.wait()
        wg_s[...] = wg_f[wsl].astype(BF16)
        wu_s[...] = wu_f[wsl].astype(BF16)
        wd_s[...] = wd_f[wsl].astype(BF16)

    def compute(sl, prefetch):
        wait_rows(sl)
        x = _load_slabs(x_s, nrows, nslab, lead=(sl,)).astype(BF16)
        if prefetch:
            for r in range(nrows):
                row_copy(idxn_ref, r, (sl + ROW_AHEAD) % ROW_SLOTS).start()
        hid = _silu(_dot(x, wg_s[...])) * _dot(x, wu_s[...])
        _store_slabs(y_ref, _dot(hid.astype(BF16), wd_s[...]))

    for sl in range(ROW_SLOTS):
        @pl.when(jnp.logical_and(has_next, slot == sl))
        def _(sl=sl):
            compute(sl, True)

        @pl.when(jnp.logical_and(jnp.logical_and(active, jnp.logical_not(has_next)), slot == sl))
        def _(sl=sl):
            compute(sl, False)

    @pl.when(jnp.logical_not(active))
    def _():
        y_ref[...] = jnp.zeros(y_ref.shape, F32)


def _experts(h2, row_tok, block_e, n_active, w_gate, w_up, w_down, layer):
    d, de = w_gate.shape[2], w_gate.shape[3]
    nslab = d // LANES
    nblk = row_tok.shape[0] // EXPERT_BLOCK
    ids = (row_tok * nslab).reshape(nblk, 1, EXPERT_BLOCK)
    blk = jnp.arange(nblk, dtype=I32)
    first = jnp.logical_and(blk < n_active[0],
                            jnp.concatenate([jnp.ones((1,), bool), block_e[1:] != block_e[:-1]]))
    wslot = lax.rem(jnp.cumsum(first.astype(I32)) - 1, 2).astype(I32)
    first_at = jnp.where(first, blk, nblk)
    next_first = lax.cummin(jnp.concatenate([first_at[1:], jnp.full((1,), nblk, I32)]), reverse=True)
    nxt_e = jnp.where(next_first < nblk, block_e[jnp.minimum(next_first, nblk - 1)], -1).astype(I32)
    any_spec = pl.BlockSpec(memory_space=pl.ANY)
    id_specs = [pl.BlockSpec((1, 1, EXPERT_BLOCK), lambda i, *_, a=a: (jnp.minimum(i + a, nblk - 1), 0, 0),
                             memory_space=pltpu.SMEM) for a in range(ROW_AHEAD + 1)]
    grid_spec = pltpu.PrefetchScalarGridSpec(
        num_scalar_prefetch=5,
        grid=(nblk,),
        in_specs=id_specs + [any_spec, any_spec, any_spec, any_spec],
        out_specs=pl.BlockSpec((EXPERT_BLOCK * nslab, LANES), lambda i, *_: (i, 0)),
        scratch_shapes=[pltpu.VMEM((ROW_SLOTS, EXPERT_BLOCK * nslab, LANES), F32),
                        pltpu.VMEM((2, d, de), F32), pltpu.VMEM((2, d, de), F32), pltpu.VMEM((2, de, d), F32),
                        pltpu.VMEM((d, de), BF16), pltpu.VMEM((d, de), BF16), pltpu.VMEM((de, d), BF16),
                        pltpu.SemaphoreType.DMA((ROW_SLOTS,)), pltpu.SemaphoreType.DMA((2, 3))],
    )
    return pl.pallas_call(
        functools.partial(_experts_kernel, layer=layer),
        grid_spec=grid_spec,
        out_shape=jax.ShapeDtypeStruct((nblk * EXPERT_BLOCK * nslab, LANES), F32),
        compiler_params=_params(("arbitrary",)),
        name="moe_experts",
    )(block_e, n_active, first.astype(I32), nxt_e, wslot, *([ids] * (ROW_AHEAD + 1)), h2, w_gate, w_up, w_down)


def _combine_kernel(pos_ref, posn_ref, ys_ref, x1_ref, h2_ref, gw_ref, wsg_ref, wsu_ref, wsd_ref, gpost_ref,
                    gt_ref, o_ref, rows_s, mix_s, sems, *, tt, topk, nsteps):
    i = pl.program_id(0)
    nslab = o_ref.shape[1] // LANES
    ncopy = tt * topk

    def row_copy(ids_ref, n, sl):
        src = ys_ref.at[pl.ds(pl.multiple_of(ids_ref[0, 0, n], nslab), nslab)]
        return pltpu.make_async_copy(src, rows_s.at[sl, pl.ds(n * nslab, nslab)], sems.at[sl])

    @pl.when(i == 0)
    def _():
        def body(n, carry):
            row_copy(pos_ref, n, 0).start()
            return carry
        lax.fori_loop(0, ncopy, body, 0)

    def step(sl):
        @pl.when(i + 1 < nsteps)
        def _():
            for n in range(ncopy):
                row_copy(posn_ref, n, 1 - sl).start(priority=n % 2)

        h2 = h2_ref[...].astype(BF16)
        shared = _dot((_silu(_dot(h2, wsg_ref[...])) * _dot(h2, wsu_ref[...])).astype(BF16), wsd_ref[...])
        pltpu.make_async_copy(ys_ref.at[pl.ds(0, ncopy * nslab)], rows_s.at[sl], sems.at[sl]).wait()
        for t in range(tt):
            acc = None
            for k in range(topk):
                n = t * topk + k
                term = gw_ref[0, 0, n] * rows_s[sl, n * nslab:(n + 1) * nslab, :]
                acc = term if acc is None else acc + term
            mix_s[t * nslab:(t + 1) * nslab, :] = acc
        f = shared + _load_slabs(mix_s, tt, nslab)
        o_ref[...] = x1_ref[...] + gt_ref[0] * (_rms(f) * gpost_ref[...])

    slot = lax.rem(i, 2)
    for sl in (0, 1):
        @pl.when(slot == sl)
        def _(sl=sl):
            step(sl)


def _combine(ys, pos, x1, h2, gw, wsg, wsu, wsd, gpost, gt, tt):
    t, d = x1.shape
    topk = gw.shape[1]
    de = wsg.shape[1]
    nslab = d // LANES
    nsteps = t // tt
    scalars = pl.BlockSpec((1, 1, tt * topk), lambda i: (i, 0, 0), memory_space=pltpu.SMEM)
    scalars_next = pl.BlockSpec((1, 1, tt * topk), lambda i: (jnp.minimum(i + 1, nsteps - 1), 0, 0),
                                memory_space=pltpu.SMEM)
    pos_slab = (pos * nslab).reshape(nsteps, 1, tt * topk)
    return pl.pallas_call(
        functools.partial(_combine_kernel, tt=tt, topk=topk, nsteps=nsteps),
        grid=(nsteps,),
        in_specs=[scalars, scalars_next,
                  pl.BlockSpec(memory_space=pl.ANY),
                  pl.BlockSpec((tt, d), lambda i: (i, 0)),
                  pl.BlockSpec((tt, d), lambda i: (i, 0)),
                  scalars,
                  pl.BlockSpec((d, de), lambda i: (0, 0)),
                  pl.BlockSpec((d, de), lambda i: (0, 0)),
                  pl.BlockSpec((de, d), lambda i: (0, 0)),
                  pl.BlockSpec((1, d), lambda i: (0, 0)),
                  _mod_spec(gt, t, tt, 0)],
        out_specs=pl.BlockSpec((tt, d), lambda i: (i, 0)),
        out_shape=jax.ShapeDtypeStruct((t, d), F32),
        scratch_shapes=[pltpu.VMEM((2, tt * topk * nslab, LANES), F32), pltpu.VMEM((tt * nslab, LANES), F32),
                        pltpu.SemaphoreType.DMA((2,))],
        compiler_params=_params(("arbitrary",)),
        name="moe_combine",
    )(pos_slab, pos_slab, ys, x1, h2, gw.reshape(nsteps, 1, tt * topk), wsg, wsu, wsd, gpost, gt)


def _mod_parts(mod, nb_p, reps):
    parts = jnp.split(mod, 6, axis=-1)
    prompt = [m[:nb_p, None, :] for m in parts]
    sample = [jnp.repeat(m[nb_p:], reps, axis=0)[None] for m in parts]
    return prompt, sample


def kernel(x_prompt, x_sample, cache_k, cache_v, cache_logf, state_conv, state_mlstm_C, state_mlstm_n,
           state_mlstm_m, page_table, c_prompt, c_sample, w_ada, b_ada, g_pre_mix, g_post_mix, g_pre_ffn,
           g_post_ffn, w_in, w_out, conv_dw, conv_db, conv_ln_g, conv_ln_b, mlstm_i_bias, mlstm_f_bias,
           fox_f_bias, w_router, router_bias, w_exp_gate, w_exp_up, w_exp_down, w_sh_gate, w_sh_up, w_sh_down):
    bp, lp, d = x_prompt.shape
    bs, ls, _ = x_sample.shape
    depth = w_ada.shape[0]
    cc = conv_dw.shape[2]
    mh = state_mlstm_C.shape[2]
    mw = mh * HEAD_DIM
    ah = cache_k.shape[3]
    aw = ah * HEAD_DIM
    n_phys, page = cache_k.shape[1], cache_k.shape[2]
    n_pages = page_table.shape[1]
    tp, ts = bp * lp, bs * ls
    n_main = 2 * cc + 4 * mw
    g0 = n_main
    a0 = g0 + 2 * mh
    f0 = a0 + 3 * aw

    xp = x_prompt.reshape(tp, d)
    xs = x_sample.reshape(ts, d)
    mod_all = _ada(jnp.concatenate([c_prompt, c_sample], axis=0), w_ada, b_ada)
    cache_k2 = cache_k.reshape(depth, n_phys, page * ah, HEAD_DIM)
    cache_v2 = cache_v.reshape(depth, n_phys, page * ah, HEAD_DIM)
    cache_lft = jnp.swapaxes(cache_logf, 2, 3)
    zero_conv = jnp.zeros((bp, CONV_W - 1, cc), F32)
    zero_c = jnp.zeros((bp, mh, HEAD_DIM, HEAD_DIM), F32)
    zero_n = jnp.zeros((bp, mh, HEAD_DIM), F32)
    zero_m = jnp.zeros((bp, mh), F32)
    tm_s = ts

    outs = {k: [] for k in ("kp", "vp", "lfp", "ks", "vs", "lfs", "convp", "convs",
                            "cp", "np", "mp", "cs", "ns", "ms")}
    for l in range(depth):
        (sh1p, sc1p, gt1p, sh2p, sc2p, gt2p), (sh1s, sc1s, gt1s, sh2s, sc2s, gt2s) = _mod_parts(mod_all[l], bp, ls)
        w_main = jnp.concatenate([w_in[l][:, :n_main], w_in[l][:, a0:f0]], axis=1).astype(BF16)
        gate_w = jnp.concatenate([w_in[l][:, g0:a0], w_in[l][:, f0:]], axis=1)
        gate_w = jnp.pad(gate_w, ((0, 0), (0, GATE_COLS - gate_w.shape[1]))).astype(BF16)
        gate_bias = jnp.concatenate([mlstm_i_bias[l], mlstm_f_bias[l], fox_f_bias[l]])
        bias_row = jnp.pad(gate_bias, (0, GATE_COLS - gate_bias.shape[0]))[None, :]
        bias_col = gate_bias[:2 * mh, None]
        w_out_b = w_out[l].astype(BF16)
        w_router_t = w_router[l].T
        wsg, wsu, wsd = w_sh_gate[l].astype(BF16), w_sh_up[l].astype(BF16), w_sh_down[l].astype(BF16)
        gpm, gqm = g_pre_mix[l][None, :], g_post_mix[l][None, :]
        gpf, gqf = g_pre_ffn[l][None, :], g_post_ffn[l][None, :]
        dw, db = conv_dw[l], conv_db[l][None, :]
        lng, lnb = conv_ln_g[l][None, :], conv_ln_b[l][None, :]

        groups = []
        for (x, sc1, sh1, tm) in ((xp, sc1p, sh1p, 512), (xs, sc1s, sh1s, tm_s)):
            u = _inproj(x, gpm, sc1, sh1, w_main, tm, 1024)
            gates = _inproj(x, gpm, sc1, sh1, gate_w, tm, GATE_COLS)
            groups.append((u, gates))
        (u_p, gates_p), (u_s, gates_s) = groups

        yc_p, conv_p = _conv(u_p, zero_conv, dw, db, lng, lnb, bp, lp, min(512, lp))
        yc_s, conv_s = _conv(u_s, state_conv[l], dw, db, lng, lnb, bs, ls, ls)

        def gates_t(gates, nb, seq):
            c = MLSTM_CHUNK if seq % MLSTM_CHUNK == 0 else seq
            return gates[:, :2 * mh].reshape(nb, seq // c, c, 2 * mh).transpose(0, 1, 3, 2)

        mcol0 = (2 * cc) // mw
        ym_p, c_p, n_p, m_p = _mlstm(u_p, gates_p, gates_t(gates_p, bp, lp), bias_row, bias_col,
                                     zero_c, zero_n, zero_m, bp, lp, mcol0)
        ym_s, c_s, n_s, m_s = _mlstm(u_s, gates_s, gates_t(gates_s, bs, ls), bias_row, bias_col,
                                     state_mlstm_C[l], state_mlstm_n[l], state_mlstm_m[l], bs, ls, mcol0)

        fb = fox_f_bias[l]
        lf_p = _log_sigmoid(gates_p[:, 2 * mh:2 * mh + ah] + fb).reshape(bp, lp, ah)
        lf_s = _log_sigmoid(gates_s[:, 2 * mh:2 * mh + ah] + fb).reshape(bs, ls, ah)
        fcum = jnp.cumsum(lf_p, axis=1).transpose(0, 2, 1)
        qcol = n_main // HEAD_DIM
        ya_p = _fox_prompt(u_p, fcum[..., None], fcum[:, :, None, :], bp, lp, ah,
                           qcol, qcol + ah, qcol + 2 * ah)
        k_s = u_s[:, n_main + aw:n_main + 2 * aw]
        v_s = u_s[:, n_main + 2 * aw:n_main + 3 * aw]
        past = _past_bias(cache_lft, page_table, l)
        past = past.transpose(0, 2, 1).reshape(bs, n_pages, 1, page * ah)
        ya_s = _fox_sample(u_s[:, n_main:n_main + aw].reshape(ts * ah, HEAD_DIM),
                           k_s.reshape(ts * ah, HEAD_DIM), v_s.reshape(ts * ah, HEAD_DIM),
                           lf_s.reshape(ts * ah, 1), lf_s.reshape(bs, 1, ls * ah),
                           cache_k2, cache_v2, past, page_table, l, ah).reshape(ts, aw)

        x1_p, h2_p, h2slab_p, sco_p = _outproj(yc_p, ym_p, ya_p, xp, w_out_b, gqm, gt1p, gpf, sc2p, sh2p,
                                               w_router_t, 256)
        x1_s, h2_s, h2slab_s, sco_s = _outproj(yc_s, ym_s, ya_s, xs, w_out_b, gqm, gt1s, gpf, sc2s, sh2s,
                                               w_router_t, tm_s)

        idx_t, gw_t, rank_t, counts = _route(jnp.concatenate([sco_p, sco_s], axis=1), router_bias[l])
        row_tok, pos, block_e, n_active = _dispatch(idx_t, rank_t, counts)
        h2_all = jnp.concatenate([h2slab_p, h2slab_s], axis=0)
        ysort = _experts(h2_all, row_tok, block_e, n_active, w_exp_gate, w_exp_up, w_exp_down, l)
        gw = gw_t.T
        xp = _combine(ysort, pos[:tp], x1_p, h2_p, gw[:tp], wsg, wsu, wsd, gqf, gt2p, 128)
        xs = _combine(ysort, pos[tp:], x1_s, h2_s, gw[tp:], wsg, wsu, wsd, gqf, gt2s, min(128, ts))

        outs["kp"].append(u_p[:, n_main + aw:n_main + 2 * aw].reshape(bp, lp, ah, HEAD_DIM))
        outs["vp"].append(u_p[:, n_main + 2 * aw:n_main + 3 * aw].reshape(bp, lp, ah, HEAD_DIM))
        outs["lfp"].append(lf_p)
        outs["ks"].append(k_s.reshape(bs, ls, ah, HEAD_DIM))
        outs["vs"].append(v_s.reshape(bs, ls, ah, HEAD_DIM))
        outs["lfs"].append(lf_s)
        outs["convp"].append(conv_p)
        outs["convs"].append(conv_s)
        outs["cp"].append(c_p)
        outs["np"].append(n_p)
        outs["mp"].append(m_p.reshape(bp, mh))
        outs["cs"].append(c_s)
        outs["ns"].append(n_s)
        outs["ms"].append(m_s.reshape(bs, mh))

    st = {k: jnp.stack(v) for k, v in outs.items()}
    return (xp.reshape(bp, lp, d), xs.reshape(bs, ls, d),
            st["kp"], st["vp"], st["lfp"], st["ks"], st["vs"], st["lfs"],
            st["convp"], st["convs"], st["cp"], st["np"], st["mp"], st["cs"], st["ns"], st["ms"])
```
